```python
import math
import jax, jax.numpy as jnp
from jax import lax
import numpy as np

D_MODEL = 2048
BATCH = 2
SEQ = 8192
DEPTH = 4

HEAD_DIM = 128
FOX_HEADS = D_MODEL // (2 * HEAD_DIM)
DIFF_HEADS = D_MODEL // (4 * HEAD_DIM)
FOX_WIDTH = FOX_HEADS * HEAD_DIM
DIFF_QK_WIDTH = 2 * DIFF_HEADS * HEAD_DIM
DIFF_V_WIDTH = DIFF_HEADS * 2 * HEAD_DIM
MIX_WIDTH = FOX_WIDTH + DIFF_V_WIDTH
IN_WIDTH = 3 * FOX_WIDTH + FOX_HEADS + 2 * DIFF_QK_WIDTH + DIFF_V_WIDTH
ROPE_THETA = 500000.0
ROPE_DIM = HEAD_DIM // 4
Q_BLOCK = 128
N_GROUPS = 4
EXPERTS_PER_GROUP = 8
N_EXPERTS = N_GROUPS * EXPERTS_PER_GROUP
TOP_K = 2
D_EXPERT = D_MODEL // 4
DISPATCH_CHUNK = 256
DEEPNORM_ALPHA = (2.0 * DEPTH) ** 0.25
DEEPNORM_BETA = (8.0 * DEPTH) ** -0.25
LN_EPS = 1e-5
RMS_EPS = 1e-5

kernel_name = "fox_diffattn_hiermoe_deepnorm_trunk"


def _layer_norm(x, g, b):
    xf = x.astype(jnp.float32)
    mu = jnp.mean(xf, axis=-1, keepdims=True)
    var = jnp.mean(jnp.square(xf - mu), axis=-1, keepdims=True)
    y = (xf - mu) * lax.rsqrt(var + LN_EPS)
    return (y * g.astype(jnp.float32) + b.astype(jnp.float32)).astype(x.dtype)


def _rms_norm(x, g):
    xf = x.astype(jnp.float32)
    y = xf * lax.rsqrt(jnp.mean(jnp.square(xf), axis=-1, keepdims=True) + RMS_EPS)
    return (y * g.astype(jnp.float32)).astype(x.dtype)


def _rope_tables(seq):
    pos = jnp.arange(seq, dtype=jnp.float32)
    inv_freq = ROPE_THETA ** (-jnp.arange(0, ROPE_DIM, 2, dtype=jnp.float32) / ROPE_DIM)
    ang = pos[:, None] * inv_freq[None, :]
    return jnp.cos(ang), jnp.sin(ang)


def _partial_rope(x, cos, sin):
    xr, xp = x[..., :ROPE_DIM], x[..., ROPE_DIM:]
    x1, x2 = jnp.split(xr, 2, axis=-1)
    c = cos.astype(x.dtype)
    s = sin.astype(x.dtype)
    rot = jnp.concatenate([x1 * c - x2 * s, x2 * c + x1 * s], axis=-1)
    return jnp.concatenate([rot, xp], axis=-1)


def _to_qblocks(a):
    b, h, s = a.shape[:3]
    return jnp.moveaxis(a.reshape(b, h, s // Q_BLOCK, Q_BLOCK, *a.shape[3:]), 2, 0)


def _from_qblocks(o):
    nb, b, h, qb, dv = o.shape
    return jnp.moveaxis(o, 0, 2).reshape(b, h, nb * qb, dv)


def _fox_attention(q, k, v, log_f):
    seq = q.shape[2]
    scale = HEAD_DIM ** -0.5
    cum = jnp.cumsum(log_f, axis=-1)
    kpos = jnp.arange(seq)
    qpos = kpos.reshape(seq // Q_BLOCK, Q_BLOCK)

    def block(args):
        q_b, cum_b, pos_b = args
        s = jnp.einsum('bhqd,bhkd->bhqk', q_b, k).astype(jnp.float32) * scale
        s = s + cum_b[..., :, None] - cum[:, :, None, :]
        s = jnp.where(kpos[None, :] <= pos_b[:, None], s, -jnp.inf)
        p = jax.nn.softmax(s, axis=-1)
        return jnp.einsum('bhqk,bhkd->bhqd', p.astype(v.dtype), v)

    return _from_qblocks(lax.map(block, (_to_qblocks(q), _to_qblocks(cum), qpos)))


def _diff_attention(q1, q2, k1, k2, v, lam):
    seq = q1.shape[2]
    scale = HEAD_DIM ** -0.5
    kpos = jnp.arange(seq)
    qpos = kpos.reshape(seq // Q_BLOCK, Q_BLOCK)

    def block(args):
        q1_b, q2_b, pos_b = args
        mask = kpos[None, :] <= pos_b[:, None]
        s1 = jnp.einsum('bhqd,bhkd->bhqk', q1_b, k1).astype(jnp.float32) * scale
        s2 = jnp.einsum('bhqd,bhkd->bhqk', q2_b, k2).astype(jnp.float32) * scale
        p1 = jax.nn.softmax(jnp.where(mask, s1, -jnp.inf), axis=-1)
        p2 = jax.nn.softmax(jnp.where(mask, s2, -jnp.inf), axis=-1)
        a = p1 - lam * p2
        return jnp.einsum('bhqk,bhkd->bhqd', a.astype(v.dtype), v)

    return _from_qblocks(lax.map(block, (_to_qblocks(q1), _to_qblocks(q2), qpos)))


def _hybrid_mixer(h, w_in, b_f, lam_vecs, subln_g, w_o, cos, sin, lambda_init):
    bsz, seq, _ = h.shape
    proj = h @ w_in
    o1 = FOX_WIDTH
    o2 = o1 + FOX_WIDTH
    o3 = o2 + FOX_WIDTH
    o4 = o3 + FOX_HEADS
    o5 = o4 + DIFF_QK_WIDTH
    o6 = o5 + DIFF_QK_WIDTH
    fq, fk, fv, ff = proj[..., :o1], proj[..., o1:o2], proj[..., o2:o3], proj[..., o3:o4]
    dq, dk, dv = proj[..., o4:o5], proj[..., o5:o6], proj[..., o6:]

    def heads(a, n):
        return a.reshape(bsz, seq, n, -1).transpose(0, 2, 1, 3)

    log_f = jax.nn.log_sigmoid(ff.astype(jnp.float32) + b_f.astype(jnp.float32)).transpose(0, 2, 1)
    fox_out = _fox_attention(heads(fq, FOX_HEADS), heads(fk, FOX_HEADS), heads(fv, FOX_HEADS), log_f)

    dq = dq.reshape(bsz, seq, DIFF_HEADS, 2, HEAD_DIM)
    dk = dk.reshape(bsz, seq, DIFF_HEADS, 2, HEAD_DIM)
    q1 = _partial_rope(dq[:, :, :, 0].transpose(0, 2, 1, 3), cos, sin)
    q2 = _partial_rope(dq[:, :, :, 1].transpose(0, 2, 1, 3), cos, sin)
    k1 = _partial_rope(dk[:, :, :, 0].transpose(0, 2, 1, 3), cos, sin)
    k2 = _partial_rope(dk[:, :, :, 1].transpose(0, 2, 1, 3), cos, sin)
    lv = lam_vecs.astype(jnp.float32)
    lam = jnp.exp(jnp.sum(lv[0] * lv[1])) - jnp.exp(jnp.sum(lv[2] * lv[3])) + lambda_init
    diff_out = _diff_attention(q1, q2, k1, k2, heads(dv, DIFF_HEADS), lam)
    diff_out = _rms_norm(diff_out, subln_g) * (1.0 - lambda_init)

    merged = jnp.concatenate([
        fox_out.transpose(0, 2, 1, 3).reshape(bsz, seq, FOX_WIDTH),
        diff_out.transpose(0, 2, 1, 3).reshape(bsz, seq, DIFF_V_WIDTH)], axis=-1)
    return merged @ w_o


def _hierarchical_moe(h, w_rg, b_rg, w_re, b_re, w_gate, w_up, w_down):
    bsz, seq, dm = h.shape
    n_tok = bsz * seq
    xt = h.reshape(n_tok, dm)
    g_logits = (xt @ w_rg).astype(jnp.float32) + b_rg.astype(jnp.float32)
    g_prob = jax.nn.softmax(g_logits, axis=-1)
    g_sel = jnp.argmax(g_logits, axis=-1)
    g_w = jnp.take_along_axis(g_prob, g_sel[:, None], axis=-1)
    e_logits = ((xt @ w_re).astype(jnp.float32) + b_re.astype(jnp.float32)).reshape(n_tok, N_GROUPS, EXPERTS_PER_GROUP)
    e_logits = jnp.take_along_axis(e_logits, g_sel[:, None, None], axis=1)[:, 0]
    e_prob = jax.nn.softmax(e_logits, axis=-1)
    top_w, top_i = lax.top_k(e_prob, TOP_K)
    top_w = top_w / jnp.sum(top_w, axis=-1, keepdims=True)
    gate = g_w * top_w
    expert_id = g_sel[:, None] * EXPERTS_PER_GROUP + top_i

    n_assign = n_tok * TOP_K
    chunk = DISPATCH_CHUNK
    n_slots = -(-n_assign // chunk) * chunk + N_EXPERTS * chunk
    n_chunks = n_slots // chunk
    eid = expert_id.reshape(n_assign).astype(jnp.int32)
    tok = jnp.repeat(jnp.arange(n_tok, dtype=jnp.int32), TOP_K)
    wts = gate.reshape(n_assign)
    order = jnp.argsort(eid)
    eid_s, tok_s, w_s = eid[order], tok[order], wts[order]
    counts = jnp.bincount(eid, length=N_EXPERTS)
    starts = jnp.cumsum(counts) - counts
    padded = (counts + chunk - 1) // chunk * chunk
    pends = jnp.cumsum(padded)
    pstarts = pends - padded
    dest = pstarts[eid_s] + (jnp.arange(n_assign, dtype=jnp.int32) - starts[eid_s])
    slot_tok = jnp.full((n_slots,), n_tok, dtype=jnp.int32).at[dest].set(tok_s)
    slot_w = jnp.zeros((n_slots,), jnp.float32).at[dest].set(w_s)
    chunk_start = jnp.arange(n_chunks, dtype=jnp.int32) * chunk
    chunk_expert = jnp.minimum(jnp.sum(chunk_start[:, None] >= pends[None, :], axis=1), N_EXPERTS - 1).astype(jnp.int32)
    x_pad = jnp.concatenate([xt, jnp.zeros((1, dm), xt.dtype)], axis=0)
    xs = x_pad[slot_tok].reshape(n_chunks, chunk, dm)

    def expert_chunk(args):
        xc, e = args
        hid = jax.nn.silu(xc @ w_gate[e]) * (xc @ w_up[e])
        return hid @ w_down[e]

    ys = lax.map(expert_chunk, (xs, chunk_expert)).reshape(n_slots, dm)
    ys = ys * slot_w[:, None].astype(ys.dtype)
    out = jnp.zeros((n_tok + 1, dm), ys.dtype).at[slot_tok].add(ys)[:n_tok]
    return out.reshape(bsz, seq, dm)


def setup_inputs(seed: int = 0) -> dict:
    key = jax.random.key(seed)
    ks = jax.random.split(key, 18)
    f32 = jnp.float32
    col_scale = jnp.concatenate([
        jnp.ones((2 * FOX_WIDTH,), f32), jnp.full((FOX_WIDTH,), DEEPNORM_BETA, f32),
        jnp.ones((FOX_HEADS + 2 * DIFF_QK_WIDTH,), f32), jnp.full((DIFF_V_WIDTH,), DEEPNORM_BETA, f32)])
    x = jax.random.normal(ks[0], (BATCH, SEQ, D_MODEL), f32)
    w_in = jax.random.normal(ks[1], (DEPTH, D_MODEL, IN_WIDTH), f32) * (D_MODEL ** -0.5) * col_scale
    b_f = jax.random.uniform(ks[2], (DEPTH, FOX_HEADS), f32, 1.0, 4.0)
    diff_lambda = jax.random.normal(ks[3], (DEPTH, 4, HEAD_DIM), f32) * 0.1
    diff_subln_g = 1.0 + 0.02 * jax.random.normal(ks[4], (DEPTH, 2 * HEAD_DIM), f32)
    w_o = jax.random.normal(ks[5], (DEPTH, MIX_WIDTH, D_MODEL), f32) * (MIX_WIDTH ** -0.5) * DEEPNORM_BETA
    ln1_g = 1.0 + 0.02 * jax.random.normal(ks[6], (DEPTH, D_MODEL), f32)
    ln1_b = 0.02 * jax.random.normal(ks[7], (DEPTH, D_MODEL), f32)
    w_router_group = jax.random.normal(ks[8], (DEPTH, D_MODEL, N_GROUPS), f32) * (D_MODEL ** -0.5)
    b_router_group = 0.01 * jax.random.normal(ks[9], (DEPTH, N_GROUPS), f32)
    w_router_expert = jax.random.normal(ks[10], (DEPTH, D_MODEL, N_EXPERTS), f32) * (D_MODEL ** -0.5)
    b_router_expert = 0.01 * jax.random.normal(ks[11], (DEPTH, N_EXPERTS), f32)
    w_gate = jax.random.normal(ks[12], (DEPTH, N_EXPERTS, D_MODEL, D_EXPERT), f32) * (D_MODEL ** -0.5)
    w_up = jax.random.normal(ks[13], (DEPTH, N_EXPERTS, D_MODEL, D_EXPERT), f32) * (D_MODEL ** -0.5)
    w_down = jax.random.normal(ks[14], (DEPTH, N_EXPERTS, D_EXPERT, D_MODEL), f32) * (D_EXPERT ** -0.5) * DEEPNORM_BETA
    ln2_g = 1.0 + 0.02 * jax.random.normal(ks[15], (DEPTH, D_MODEL), f32)
    ln2_b = 0.02 * jax.random.normal(ks[16], (DEPTH, D_MODEL), f32)
    return {"x": x, "w_in": w_in, "b_f": b_f, "diff_lambda": diff_lambda, "diff_subln_g": diff_subln_g,
            "w_o": w_o, "ln1_g": ln1_g, "ln1_b": ln1_b, "w_router_group": w_router_group,
            "b_router_group": b_router_group, "w_router_expert": w_router_expert,
            "b_router_expert": b_router_expert, "w_gate": w_gate, "w_up": w_up, "w_down": w_down,
            "ln2_g": ln2_g, "ln2_b": ln2_b}


def reference(x, w_in, b_f, diff_lambda, diff_subln_g, w_o, ln1_g, ln1_b, w_router_group,
              b_router_group, w_router_expert, b_router_expert, w_gate, w_up, w_down, ln2_g, ln2_b):
    cos, sin = _rope_tables(x.shape[1])
    for l in range(DEPTH):
        lambda_init = 0.8 - 0.6 * math.exp(-0.3 * l)
        mix = _hybrid_mixer(x, w_in[l], b_f[l], diff_lambda[l], diff_subln_g[l], w_o[l], cos, sin, lambda_init)
        x = _layer_norm(DEEPNORM_ALPHA * x + mix, ln1_g[l], ln1_b[l])
        ffn = _hierarchical_moe(x, w_router_group[l], b_router_group[l], w_router_expert[l],
                                b_router_expert[l], w_gate[l], w_up[l], w_down[l])
        x = _layer_norm(DEEPNORM_ALPHA * x + ffn, ln2_g[l], ln2_b[l])
    return x
```

```python
import functools
import math

import jax
import jax.numpy as jnp
from jax import lax
from jax.experimental import pallas as pl
from jax.experimental.pallas import tpu as pltpu

F32 = jnp.float32
BF16 = jnp.bfloat16
I32 = jnp.int32

HEAD_DIM = 128
LANES = 128
FOX_HEADS = 8
DIFF_HEADS = 4
ROPE_DIM = HEAD_DIM // 4
ROPE_HALF = ROPE_DIM // 2
ROPE_THETA = 500000.0
N_GROUPS = 4
EXPERTS_PER_GROUP = 8
N_EXPERTS = N_GROUPS * EXPERTS_PER_GROUP
TOP_K = 2
LN_EPS = 1e-5
RMS_EPS = 1e-5
ATTN_SCALE = HEAD_DIM ** -0.5
EXPERT_CHUNK = 256
MIB = 1024 * 1024

NT_DIMS = (((1,), (1,)), ((), ()))


def _lane_tile(a, reps, axis=1):
    assert axis == 1
    return a if reps == 1 else jnp.concatenate([a] * reps, axis=1)


def _params(n_axes, vmem_mib):
    return pltpu.CompilerParams(
        dimension_semantics=("arbitrary",) * n_axes,
        vmem_limit_bytes=vmem_mib * MIB)


def _inproj_kernel(x_ref, w_ref, c_ref, sn_ref, sp_ref, o_ref, *, rope_lo, rope_hi):
    j = pl.program_id(1)
    acc = jnp.dot(x_ref[...], w_ref[...], preferred_element_type=F32)
    is_rope = jnp.logical_and(j >= rope_lo, j < rope_hi)

    @pl.when(is_rope)
    def _():
        c = c_ref[...]
        sn = sn_ref[...]
        sp = sp_ref[...]
        for h in range(acc.shape[1] // HEAD_DIM):
            xh = acc[:, h * HEAD_DIM:(h + 1) * HEAD_DIM]
            up = pltpu.roll(xh, HEAD_DIM - ROPE_HALF, axis=1)
            dn = pltpu.roll(xh, ROPE_HALF, axis=1)
            o_ref[:, h * HEAD_DIM:(h + 1) * HEAD_DIM] = (xh * c + up * sn + dn * sp).astype(o_ref.dtype)

    @pl.when(jnp.logical_not(is_rope))
    def _():
        o_ref[...] = acc.astype(o_ref.dtype)


def _inproj(xb, w_main, rope_c, rope_sn, rope_sp, seq, rope_cols):
    n, d = xb.shape
    width = w_main.shape[1]
    tm, tn = 1024, 512
    seq_tiles = seq // tm
    kern = functools.partial(_inproj_kernel, rope_lo=rope_cols[0] // tn, rope_hi=rope_cols[1] // tn)
    tab_spec = pl.BlockSpec((tm, LANES), lambda i, j: (i % seq_tiles, 0))
    return pl.pallas_call(
        kern,
        grid=(n // tm, width // tn),
        in_specs=[pl.BlockSpec((tm, d), lambda i, j: (i, 0)),
                  pl.BlockSpec((d, tn), lambda i, j: (0, j)),
                  tab_spec, tab_spec, tab_spec],
        out_specs=pl.BlockSpec((tm, tn), lambda i, j: (i, j)),
        out_shape=jax.ShapeDtypeStruct((n, width), BF16),
        compiler_params=_params(2, 48),
        name="inproj",
    )(xb, w_main, rope_c, rope_sn, rope_sp)


def _split3_f32(a):
    hi = a.astype(BF16).astype(F32)
    r1 = a - hi
    mid = r1.astype(BF16).astype(F32)
    lo = (r1 - mid).astype(BF16).astype(F32)
    return hi, mid, lo


def _gate_kernel(x_ref, w_ref, b_ref, qa_ref, ka_ref, tri_ref, carry_ref, *, seq_tiles):
    i = pl.program_id(0)
    tm = x_ref.shape[0]

    @pl.when(i == 0)
    def _():
        row = lax.broadcasted_iota(I32, (tm, tm), 0)
        col = lax.broadcasted_iota(I32, (tm, tm), 1)
        tri_ref[...] = jnp.where(col <= row, 1.0, 0.0).astype(BF16)

    @pl.when(i % seq_tiles == 0)
    def _():
        carry_ref[...] = jnp.zeros(carry_ref.shape, F32)

    z = jnp.dot(x_ref[...], w_ref[...], preferred_element_type=F32) + b_ref[...]
    log_f = -(jnp.maximum(-z, 0.0) + jnp.log1p(jnp.exp(-jnp.abs(z))))
    tri = tri_ref[...]
    cum = carry_ref[0:1, :]
    for piece in _split3_f32(log_f):
        cum = cum + jnp.dot(tri, piece.astype(BF16), preferred_element_type=F32)
    carry_ref[0:1, :] = cum[tm - 1:tm, :]
    c = cum * (1.0 / ATTN_SCALE)
    lane = lax.broadcasted_iota(I32, (tm, LANES), 1)
    for h in range(qa_ref.shape[1] // LANES):
        hi, mid, lo = _split3_f32(jnp.broadcast_to(c[:, h:h + 1], (tm, LANES)))
        qa = jnp.where(lane == 0, hi, jnp.where(lane == 1, mid, jnp.where(lane == 2, lo,
                       jnp.where(lane < 6, 1.0, 0.0))))
        ka = jnp.where(lane < 3, 1.0, jnp.where(lane == 3, -hi, jnp.where(lane == 4, -mid,
                       jnp.where(lane == 5, -lo, 0.0))))
        qa_ref[:, h * LANES:(h + 1) * LANES] = qa.astype(BF16)
        ka_ref[:, h * LANES:(h + 1) * LANES] = ka.astype(BF16)


def _gate(xb, w_f, b_f, seq):
    n, d = xb.shape
    tm = 1024
    width = FOX_HEADS * LANES
    kern = functools.partial(_gate_kernel, seq_tiles=seq // tm)
    out_spec = pl.BlockSpec((tm, width), lambda i: (i, 0))
    return pl.pallas_call(
        kern,
        grid=(n // tm,),
        in_specs=[pl.BlockSpec((tm, d), lambda i: (i, 0)),
                  pl.BlockSpec((d, LANES), lambda i: (0, 0)),
                  pl.BlockSpec((1, LANES), lambda i: (0, 0))],
        out_specs=[out_spec, out_spec],
        out_shape=[jax.ShapeDtypeStruct((n, width), BF16), jax.ShapeDtypeStruct((n, width), BF16)],
        scratch_shapes=[pltpu.VMEM((tm, tm), BF16), pltpu.VMEM((8, LANES), F32)],
        compiler_params=_params(1, 32),
        name="forget_gate",
    )(xb, w_f, b_f)


def _fox_kernel(q_ref, qa_ref, k_ref, ka_ref, v_ref, o_ref, m_ref, l_ref, acc_ref, *, tq, tk):
    qi = pl.program_id(2)
    m_ref[...] = jnp.full(m_ref.shape, -jnp.inf, F32)
    l_ref[...] = jnp.zeros(l_ref.shape, F32)
    acc_ref[...] = jnp.zeros(acc_ref.shape, F32)
    q = jnp.concatenate([q_ref[...], qa_ref[...]], axis=1)
    reps = tk // LANES

    def step(j, diag_offset):
        ks = pl.multiple_of(j * tk, tk)
        kb = jnp.concatenate([k_ref[pl.ds(ks, tk), :], ka_ref[pl.ds(ks, tk), :]], axis=1)
        s = lax.dot_general(q, kb, NT_DIMS, preferred_element_type=F32)
        if diag_offset is not None:
            row = lax.broadcasted_iota(I32, (tq, tk), 0)
            col = lax.broadcasted_iota(I32, (tq, tk), 1)
            s = jnp.where(col + diag_offset <= row, s, -jnp.inf)
        m_prev = m_ref[...]
        m_new = jnp.maximum(m_prev, jnp.max(s, axis=1, keepdims=True))
        alpha = jnp.exp(ATTN_SCALE * (m_prev - m_new))
        p = jnp.exp(ATTN_SCALE * (s - _lane_tile(m_new, reps, axis=1)))
        l_ref[...] = alpha * l_ref[...] + jnp.sum(p, axis=1, keepdims=True)
        acc_ref[...] = alpha * acc_ref[...] + jnp.dot(
            p.astype(BF16), v_ref[pl.ds(ks, tk), :], preferred_element_type=F32)
        m_ref[...] = m_new

    n_full = qi * (tq // tk)

    def body(j, carry):
        step(j, None)
        return carry

    lax.fori_loop(0, n_full, body, 0)
    for jj in range(tq // tk):
        step(n_full + jj, jj * tk)
    o_ref[...] = (acc_ref[...] / l_ref[...]).astype(o_ref.dtype)


def _fox_attention(proj, qaug, kaug, batch, seq):
    n = proj.shape[0]
    tq, tk = 512, 512
    nq = seq // tq
    kern = functools.partial(_fox_kernel, tq=tq, tk=tk)
    return pl.pallas_call(
        kern,
        grid=(batch, FOX_HEADS, nq),
        in_specs=[pl.BlockSpec((tq, HEAD_DIM), lambda b, h, i: (b * nq + i, h)),
                  pl.BlockSpec((tq, HEAD_DIM), lambda b, h, i: (b * nq + i, h)),
                  pl.BlockSpec((seq, HEAD_DIM), lambda b, h, i: (b, FOX_HEADS + h)),
                  pl.BlockSpec((seq, HEAD_DIM), lambda b, h, i: (b, h)),
                  pl.BlockSpec((seq, HEAD_DIM), lambda b, h, i: (b, 2 * FOX_HEADS + h))],
        out_specs=pl.BlockSpec((tq, HEAD_DIM), lambda b, h, i: (b * nq + i, h)),
        out_shape=jax.ShapeDtypeStruct((n, FOX_HEADS * HEAD_DIM), BF16),
        scratch_shapes=[pltpu.VMEM((tq, LANES), F32),
                        pltpu.VMEM((tq, LANES), F32),
                        pltpu.VMEM((tq, HEAD_DIM), F32)],
        compiler_params=_params(3, 48),
        name="fox_attention",
    )(proj, qaug, proj, kaug, proj)


def _diff_kernel(lamv_ref, g_ref, q1_ref, q2_ref, k1_ref, k2_ref, v_ref, o_ref,
                 m1_ref, l1_ref, a1_ref, m2_ref, l2_ref, a2_ref, *, tq, tk, lambda_init):
    qi = pl.program_id(2)
    for m_ref, l_ref, a_ref in ((m1_ref, l1_ref, a1_ref), (m2_ref, l2_ref, a2_ref)):
        m_ref[...] = jnp.full(m_ref.shape, -jnp.inf, F32)
        l_ref[...] = jnp.zeros(l_ref.shape, F32)
        a_ref[...] = jnp.zeros(a_ref.shape, F32)
    reps = tk // LANES
    vreps = v_ref.shape[1] // LANES

    def one_map(q_ref, k_ref, m_ref, l_ref, a_ref, ks, v, mask):
        s = lax.dot_general(q_ref[...], k_ref[pl.ds(ks, tk), :], NT_DIMS, preferred_element_type=F32)
        if mask is not None:
            s = jnp.where(mask, s, -jnp.inf)
        m_prev = m_ref[...]
        m_new = jnp.maximum(m_prev, jnp.max(s, axis=1, keepdims=True))
        alpha = jnp.exp(ATTN_SCALE * (m_prev - m_new))
        p = jnp.exp(ATTN_SCALE * (s - _lane_tile(m_new, reps, axis=1)))
        l_ref[...] = alpha * l_ref[...] + jnp.sum(p, axis=1, keepdims=True)
        a_ref[...] = _lane_tile(alpha, vreps, axis=1) * a_ref[...] + jnp.dot(
            p.astype(BF16), v, preferred_element_type=F32)
        m_ref[...] = m_new

    def step(j, diag_offset):
        ks = pl.multiple_of(j * tk, tk)
        v = v_ref[pl.ds(ks, tk), :]
        mask = None
        if diag_offset is not None:
            row = lax.broadcasted_iota(I32, (tq, tk), 0)
            col = lax.broadcasted_iota(I32, (tq, tk), 1)
            mask = col + diag_offset <= row
        one_map(q1_ref, k1_ref, m1_ref, l1_ref, a1_ref, ks, v, mask)
        one_map(q2_ref, k2_ref, m2_ref, l2_ref, a2_ref, ks, v, mask)

    n_full = qi * (tq // tk)

    def body(j, carry):
        step(j, None)
        return carry

    lax.fori_loop(0, n_full, body, 0)
    for jj in range(tq // tk):
        step(n_full + jj, jj * tk)

    lv = lamv_ref[...]
    t1 = jnp.sum(lv[0:1, :] * lv[1:2, :], axis=1, keepdims=True)
    t2 = jnp.sum(lv[2:3, :] * lv[3:4, :], axis=1, keepdims=True)
    lam = jnp.exp(t1) - jnp.exp(t2) + lambda_init
    o1 = a1_ref[...] / _lane_tile(l1_ref[...], vreps, axis=1)
    o2 = a2_ref[...] / _lane_tile(l2_ref[...], vreps, axis=1)
    o = o1 - lam * o2
    ms = jnp.mean(o * o, axis=1, keepdims=True)
    y = o * lax.rsqrt(ms + RMS_EPS) * g_ref[...]
    o_ref[...] = (y * (1.0 - lambda_init)).astype(o_ref.dtype)


def _diff_attention(proj, lam_vecs, subln_g, layer, lambda_init, batch, seq, q_blk, k_blk, v_blk):
    n = proj.shape[0]
    tq, tk = 512, 512
    nq = seq // tq
    dv = 2 * HEAD_DIM
    kern = functools.partial(_diff_kernel, tq=tq, tk=tk, lambda_init=lambda_init)
    stat = pltpu.VMEM((tq, LANES), F32)
    accs = pltpu.VMEM((tq, dv), F32)
    return pl.pallas_call(
        kern,
        grid=(batch, DIFF_HEADS, nq),
        in_specs=[pl.BlockSpec((None, 4, HEAD_DIM), lambda b, h, i: (layer, 0, 0)),
                  pl.BlockSpec((None, 1, dv), lambda b, h, i: (layer, 0, 0)),
                  pl.BlockSpec((tq, HEAD_DIM), lambda b, h, i: (b * nq + i, q_blk + 2 * h)),
                  pl.BlockSpec((tq, HEAD_DIM), lambda b, h, i: (b * nq + i, q_blk + 2 * h + 1)),
                  pl.BlockSpec((seq, HEAD_DIM), lambda b, h, i: (b, k_blk + 2 * h)),
                  pl.BlockSpec((seq, HEAD_DIM), lambda b, h, i: (b, k_blk + 2 * h + 1)),
                  pl.BlockSpec((seq, dv), lambda b, h, i: (b, v_blk + h))],
        out_specs=pl.BlockSpec((tq, dv), lambda b, h, i: (b * nq + i, h)),
        out_shape=jax.ShapeDtypeStruct((n, DIFF_HEADS * dv), BF16),
        scratch_shapes=[stat, stat, accs, stat, stat, accs],
        compiler_params=_params(3, 48),
        name="diff_attention",
    )(lam_vecs, subln_g, proj, proj, proj, proj, proj)


def _layer_norm_rows(y, g, b):
    mu = jnp.mean(y, axis=1, keepdims=True)
    yc = y - mu
    var = jnp.mean(yc * yc, axis=1, keepdims=True)
    return yc * lax.rsqrt(var + LN_EPS) * g + b


def _outproj_ln_kernel(fa_ref, da_ref, x_ref, wf_ref, wd_ref, g_ref, b_ref, o_ref, *, alpha):
    mix = jnp.dot(fa_ref[...], wf_ref[...], preferred_element_type=F32)
    mix = mix + jnp.dot(da_ref[...], wd_ref[...], preferred_element_type=F32)
    o_ref[...] = _layer_norm_rows(alpha * x_ref[...] + mix, g_ref[...], b_ref[...])


def _outproj_ln(fox_out, diff_out, xf, w_o, ln_g, ln_b, layer, alpha):
    n, d = xf.shape
    half = fox_out.shape[1]
    tm = 256
    kern = functools.partial(_outproj_ln_kernel, alpha=alpha)
    row_spec = pl.BlockSpec((tm, d), lambda i: (i, 0))
    vec_spec = pl.BlockSpec((None, 1, d), lambda i: (layer, 0, 0))
    return pl.pallas_call(
        kern,
        grid=(n // tm,),
        in_specs=[pl.BlockSpec((tm, half), lambda i: (i, 0)),
                  pl.BlockSpec((tm, half), lambda i: (i, 0)),
                  row_spec,
                  pl.BlockSpec((half, d), lambda i: (0, 0)),
                  pl.BlockSpec((half, d), lambda i: (1, 0)),
                  vec_spec, vec_spec],
        out_specs=row_spec,
        out_shape=jax.ShapeDtypeStruct((n, d), F32),
        compiler_params=_params(1, 48),
        name="outproj_ln",
    )(fox_out, diff_out, xf, w_o, w_o, ln_g, ln_b)


def _split_bf16(a):
    hi = a.astype(BF16)
    lo = (a - hi.astype(F32)).astype(BF16)
    return hi, lo


def _router_kernel(x_ref, w_ref, b_ref, gate_ref, eid_ref):
    x_hi, x_lo = _split_bf16(x_ref[...])
    w_hi, w_lo = _split_bf16(w_ref[...])
    dg = functools.partial(lax.dot_general, dimension_numbers=NT_DIMS, preferred_element_type=F32)
    lg = dg(w_hi, x_hi) + dg(w_hi, x_lo) + dg(w_lo, x_hi) + b_ref[...]

    def first_index(vals, top):
        idx = jnp.full(top.shape, len(vals) - 1, I32)
        for j in range(len(vals) - 2, -1, -1):
            idx = jnp.where(vals[j] == top, j, idx)
        return idx

    g = [lg[i:i + 1, :] for i in range(N_GROUPS)]
    gmax = functools.reduce(jnp.maximum, g)
    gsum = functools.reduce(lambda a, b: a + b, [jnp.exp(gi - gmax) for gi in g])
    g_w = 1.0 / gsum
    sel = first_index(g, gmax)

    es = []
    for j in range(EXPERTS_PER_GROUP):
        rows = [lg[N_GROUPS + gi * EXPERTS_PER_GROUP + j:N_GROUPS + gi * EXPERTS_PER_GROUP + j + 1, :]
                for gi in range(N_GROUPS)]
        e = rows[N_GROUPS - 1]
        for gi in range(N_GROUPS - 2, -1, -1):
            e = jnp.where(sel == gi, rows[gi], e)
        es.append(e)
    emax = functools.reduce(jnp.maximum, es)
    ex = [jnp.exp(e - emax) for e in es]
    esum = functools.reduce(lambda a, b: a + b, ex)
    prob = [e / esum for e in ex]
    v1 = functools.reduce(jnp.maximum, prob)
    i1 = first_index(prob, v1)
    rest = [jnp.where(i1 == j, -1.0, prob[j]) for j in range(EXPERTS_PER_GROUP)]
    v2 = functools.reduce(jnp.maximum, rest)
    i2 = first_index(rest, v2)
    den = v1 + v2
    gate_ref[...] = jnp.zeros(gate_ref.shape, F32)
    eid_ref[...] = jnp.zeros(eid_ref.shape, I32)
    gate_ref[0:1, :] = g_w * (v1 / den)
    gate_ref[1:2, :] = g_w * (v2 / den)
    eid_ref[0:1, :] = sel * EXPERTS_PER_GROUP + i1
    eid_ref[1:2, :] = sel * EXPERTS_PER_GROUP + i2


def _router(xf, w_rt, b_r):
    n, d = xf.shape
    rows = w_rt.shape[0]
    tm = 512
    return pl.pallas_call(
        _router_kernel,
        grid=(n // tm,),
        in_specs=[pl.BlockSpec((tm, d), lambda i: (i, 0)),
                  pl.BlockSpec((rows, d), lambda i: (0, 0)),
                  pl.BlockSpec((rows, 1), lambda i: (0, 0))],
        out_specs=[pl.BlockSpec((8, tm), lambda i: (0, i)),
                   pl.BlockSpec((8, tm), lambda i: (0, i))],
        out_shape=[jax.ShapeDtypeStruct((8, n), F32), jax.ShapeDtypeStruct((8, n), I32)],
        compiler_params=_params(1, 32),
        name="router",
    )(xf, w_rt, b_r)


def _dispatch_kernel(pos_ref, x_ref, xs_ref, sem, *, tm):
    def row_copy(r, p):
        return pltpu.make_async_copy(x_ref.at[pl.ds(r, 1), :], xs_ref.at[pl.ds(p, 1), :], sem)

    def issue(r, carry):
        for k in range(TOP_K):
            row_copy(r, pos_ref[k, r]).start()
        return carry

    lax.fori_loop(0, tm, issue, 0)

    def drain(r, carry):
        for k in range(TOP_K):
            row_copy(0, 0).wait()
        return carry

    lax.fori_loop(0, tm, drain, 0)


def _dispatch(pos_tiles, xf):
    n, d = xf.shape
    tm = pos_tiles.shape[2]
    kern = functools.partial(_dispatch_kernel, tm=tm)
    return pl.pallas_call(
        kern,
        grid=(n // tm,),
        in_specs=[pl.BlockSpec((None, TOP_K, tm), lambda i: (i, 0, 0), memory_space=pltpu.SMEM),
                  pl.BlockSpec((tm, d), lambda i: (i, 0))],
        out_specs=pl.BlockSpec(memory_space=pl.ANY),
        out_shape=jax.ShapeDtypeStruct((TOP_K * n, d), F32),
        scratch_shapes=[pltpu.SemaphoreType.DMA(())],
        compiler_params=_params(1, 32),
        name="dispatch",
    )(pos_tiles, xf)


def _expert_kernel(chunk_ref, expert_ref, lo_ref, hi_ref, first_ref, nitems_ref,
                   xs_ref, wg_ref, wu_ref, wd_ref, ys_ref):
    i = pl.program_id(0)

    @pl.when(i < nitems_ref[0])
    def _():
        xb = xs_ref[...].astype(BF16)
        g = jnp.dot(xb, wg_ref[...].astype(BF16), preferred_element_type=F32)
        u = jnp.dot(xb, wu_ref[...].astype(BF16), preferred_element_type=F32)
        hid = (g * (1.0 / (1.0 + jnp.exp(-g)))) * u
        y = jnp.dot(hid.astype(BF16), wd_ref[...].astype(BF16), preferred_element_type=F32)
        row = lax.broadcasted_iota(I32, y.shape, 0)
        mine = jnp.logical_and(row >= lo_ref[i], row < hi_ref[i])

        @pl.when(first_ref[i] == 1)
        def _():
            ys_ref[...] = jnp.where(mine, y, 0.0)

        @pl.when(first_ref[i] == 0)
        def _():
            ys_ref[...] = jnp.where(mine, y, ys_ref[...])


def _experts(meta, xs, w_gate, w_up, w_down, layer):
    rows, d = xs.shape
    de = w_gate.shape[3]
    c = EXPERT_CHUNK
    n_items = meta[0].shape[0]
    grid_spec = pltpu.PrefetchScalarGridSpec(
        num_scalar_prefetch=6,
        grid=(n_items,),
        in_specs=[pl.BlockSpec((c, d), lambda i, ch, ex, lo, hi, fi, ni: (ch[i], 0)),
                  pl.BlockSpec((None, None, d, de), lambda i, ch, ex, lo, hi, fi, ni: (layer, ex[i], 0, 0)),
                  pl.BlockSpec((None, None, d, de), lambda i, ch, ex, lo, hi, fi, ni: (layer, ex[i], 0, 0)),
                  pl.BlockSpec((None, None, de, d), lambda i, ch, ex, lo, hi, fi, ni: (layer, ex[i], 0, 0))],
        out_specs=pl.BlockSpec((c, d), lambda i, ch, ex, lo, hi, fi, ni: (ch[i], 0)),
    )
    return pl.pallas_call(
        _expert_kernel,
        grid_spec=grid_spec,
        out_shape=jax.ShapeDtypeStruct((rows, d), F32),
        compiler_params=_params(1, 56),
        name="experts",
    )(*meta, xs, w_gate, w_up, w_down)


def _combine_ln_kernel(pos_ref, x_ref, g0_ref, g1_ref, gam_ref, bet_ref, ys_ref, o_ref, ob_ref,
                       y0_ref, y1_ref, sem, *, tm, alpha):
    bufs = (y0_ref, y1_ref)

    def row_copy(k, r, p):
        return pltpu.make_async_copy(ys_ref.at[pl.ds(p, 1), :], bufs[k].at[pl.ds(r, 1), :], sem)

    def issue(r, carry):
        for k in range(TOP_K):
            row_copy(k, r, pos_ref[k, r]).start()
        return carry

    lax.fori_loop(0, tm, issue, 0)

    def drain(r, carry):
        for k in range(TOP_K):
            row_copy(k, 0, 0).wait()
        return carry

    lax.fori_loop(0, tm, drain, 0)

    reps = x_ref.shape[1] // LANES
    ffn = _lane_tile(g0_ref[...], reps, axis=1) * y0_ref[...]
    ffn = ffn + _lane_tile(g1_ref[...], reps, axis=1) * y1_ref[...]
    out = _layer_norm_rows(alpha * x_ref[...] + ffn, gam_ref[...], bet_ref[...])
    o_ref[...] = out
    ob_ref[...] = out.astype(BF16)


def _combine_ln(pos_tiles, xf, gate0, gate1, ys, ln_g, ln_b, layer, alpha):
    n, d = xf.shape
    tm = pos_tiles.shape[2]
    kern = functools.partial(_combine_ln_kernel, tm=tm, alpha=alpha)
    row_spec = pl.BlockSpec((tm, d), lambda i: (i, 0))
    gate_spec = pl.BlockSpec((tm, LANES), lambda i: (i, 0))
    vec_spec = pl.BlockSpec((None, 1, d), lambda i: (layer, 0, 0))
    return pl.pallas_call(
        kern,
        grid=(n // tm,),
        in_specs=[pl.BlockSpec((None, TOP_K, tm), lambda i: (i, 0, 0), memory_space=pltpu.SMEM),
                  row_spec, gate_spec, gate_spec, vec_spec, vec_spec,
                  pl.BlockSpec(memory_space=pl.ANY)],
        out_specs=[row_spec, row_spec],
        out_shape=[jax.ShapeDtypeStruct((n, d), F32), jax.ShapeDtypeStruct((n, d), BF16)],
        scratch_shapes=[pltpu.VMEM((tm, d), F32), pltpu.VMEM((tm, d), F32),
                        pltpu.SemaphoreType.DMA(())],
        compiler_params=_params(1, 40),
        name="combine_ln",
    )(pos_tiles, xf, gate0, gate1, ln_g, ln_b, ys)


def _routing_tables(eid, n_items):
    flat = eid.reshape(-1)
    n_assign = flat.shape[0]
    c = EXPERT_CHUNK
    onehot = (flat[:, None] == jnp.arange(N_EXPERTS, dtype=I32)[None, :]).astype(I32)
    csum = jnp.cumsum(onehot, axis=0)
    counts = csum[-1]
    ends = jnp.cumsum(counts)
    starts = ends - counts
    rank = jnp.sum(csum * onehot, axis=1) - 1
    pos = jnp.sum(onehot * starts[None, :], axis=1) + rank

    first_chunk = starts // c
    last_chunk = (ends - 1) // c
    items_per = jnp.where(counts > 0, last_chunk - first_chunk + 1, 0)
    item_ends = jnp.cumsum(items_per)
    total = item_ends[-1]
    it = jnp.minimum(jnp.arange(n_items, dtype=I32), total - 1)
    ex = jnp.sum((it[:, None] >= item_ends[None, :]).astype(I32), axis=1)
    ex_oh = (ex[:, None] == jnp.arange(N_EXPERTS, dtype=I32)[None, :]).astype(I32)
    pick = lambda v: jnp.sum(ex_oh * v[None, :], axis=1)
    chunk = pick(first_chunk) + it - (pick(item_ends) - pick(items_per))
    lo = jnp.maximum(pick(starts), chunk * c) - chunk * c
    hi = jnp.minimum(pick(ends), (chunk + 1) * c) - chunk * c
    prev_chunk = jnp.concatenate([jnp.full((1,), -1, I32), chunk[:-1]])
    first = (chunk != prev_chunk).astype(I32)
    meta = (chunk.astype(I32), ex.astype(I32), lo.astype(I32), hi.astype(I32), first,
            total.reshape(1).astype(I32))
    return pos.reshape(eid.shape).astype(I32), meta


def _rope_tables(seq):
    pos = jnp.arange(seq, dtype=F32)
    inv_freq = ROPE_THETA ** (-jnp.arange(0, ROPE_DIM, 2, dtype=F32) / ROPE_DIM)
    ang = pos[:, None] * inv_freq[None, :]
    cos, sin = jnp.cos(ang), jnp.sin(ang)
    ones = jnp.ones((seq, HEAD_DIM - ROPE_DIM), F32)
    zeros_half = jnp.zeros((seq, ROPE_HALF), F32)
    zeros_rest = jnp.zeros((seq, HEAD_DIM - ROPE_DIM), F32)
    c_tab = jnp.concatenate([cos, cos, ones], axis=1)
    sn_tab = jnp.concatenate([-sin, zeros_half, zeros_rest], axis=1)
    sp_tab = jnp.concatenate([zeros_half, sin, zeros_rest], axis=1)
    return c_tab, sn_tab, sp_tab


def kernel(x, w_in, b_f, diff_lambda, diff_subln_g, w_o, ln1_g, ln1_b, w_router_group,
           b_router_group, w_router_expert, b_router_expert, w_gate, w_up, w_down, ln2_g, ln2_b):
    batch, seq, d = x.shape
    depth = w_in.shape[0]
    n = batch * seq
    alpha = (2.0 * depth) ** 0.25
    fox_w = FOX_HEADS * HEAD_DIM
    diff_qk_w = 2 * DIFF_HEADS * HEAD_DIM
    gate_lo = 3 * fox_w
    gate_hi = gate_lo + FOX_HEADS
    q_blk = 3 * FOX_HEADS
    k_blk = q_blk + 2 * DIFF_HEADS
    v_blk = (k_blk + 2 * DIFF_HEADS) // 2
    rope_cols = (gate_lo, gate_lo + 2 * diff_qk_w)

    rope_c, rope_sn, rope_sp = _rope_tables(seq)
    xf = x.reshape(n, d)
    xb = xf.astype(BF16)
    n_items = (TOP_K * n) // EXPERT_CHUNK + N_EXPERTS
    tile = 256

    for layer in range(depth):
        lambda_init = 0.8 - 0.6 * math.exp(-0.3 * layer)
        wl = w_in[layer]
        w_main = jnp.concatenate([wl[:, :gate_lo], wl[:, gate_hi:]], axis=1).astype(BF16)
        w_f = jnp.pad(wl[:, gate_lo:gate_hi], ((0, 0), (0, LANES - FOX_HEADS))).astype(BF16)
        bias_f = jnp.pad(b_f[layer], (0, LANES - FOX_HEADS)).reshape(1, LANES)

        proj = _inproj(xb, w_main, rope_c, rope_sn, rope_sp, seq, rope_cols)
        qaug, kaug = _gate(xb, w_f, bias_f, seq)
        fox_out = _fox_attention(proj, qaug, kaug, batch, seq)
        diff_out = _diff_attention(proj, diff_lambda, diff_subln_g.reshape(depth, 1, -1), layer,
                                   lambda_init, batch, seq, q_blk, k_blk, v_blk)
        x1 = _outproj_ln(fox_out, diff_out, xf, w_o[layer].astype(BF16),
                              ln1_g.reshape(depth, 1, d), ln1_b.reshape(depth, 1, d), layer, alpha)

        w_rt = jnp.concatenate([w_router_group[layer], w_router_expert[layer]], axis=1).T
        w_rt = jnp.pad(w_rt, ((0, 4), (0, 0)))
        b_r = jnp.pad(jnp.concatenate([b_router_group[layer], b_router_expert[layer]]), (0, 4)).reshape(-1, 1)
        gate8, eid8 = _router(x1, w_rt, b_r)
        pos, meta = _routing_tables(eid8[:TOP_K], n_items)
        pos_tiles = pos.reshape(TOP_K, n // tile, tile).transpose(1, 0, 2)
        xs = _dispatch(pos_tiles, x1)
        ys = _experts(meta, xs, w_gate, w_up, w_down, layer)
        gate0 = jnp.broadcast_to(gate8[0][:, None], (n, LANES))
        gate1 = jnp.broadcast_to(gate8[1][:, None], (n, LANES))
        xf, xb = _combine_ln(pos_tiles, x1, gate0, gate1, ys, ln2_g.reshape(depth, 1, d),
                             ln2_b.reshape(depth, 1, d), layer, alpha)
    return xf.reshape(batch, seq, d)
```

```python
import functools
import math

import jax
import jax.numpy as jnp
from jax import lax
from jax.experimental import pallas as pl
from jax.experimental.pallas import tpu as pltpu

F32 = jnp.float32
BF16 = jnp.bfloat16
I32 = jnp.int32

HEAD_DIM = 128
LANES = 128
FOX_HEADS = 8
DIFF_HEADS = 4
ROPE_DIM = HEAD_DIM // 4
ROPE_HALF = ROPE_DIM // 2
ROPE_THETA = 500000.0
N_GROUPS = 4
EXPERTS_PER_GROUP = 8
N_EXPERTS = N_GROUPS * EXPERTS_PER_GROUP
TOP_K = 2
LN_EPS = 1e-5
RMS_EPS = 1e-5
ATTN_SCALE = HEAD_DIM ** -0.5
EXPERT_CHUNK = 256
MIB = 1024 * 1024

NT_DIMS = (((1,), (1,)), ((), ()))


def _lane_tile(a, reps, axis=1):
    assert axis == 1
    return a if reps == 1 else jnp.concatenate([a] * reps, axis=1)


def _params(n_axes, vmem_mib):
    return pltpu.CompilerParams(
        dimension_semantics=("arbitrary",) * n_axes,
        vmem_limit_bytes=vmem_mib * MIB)


def _inproj_kernel(x_ref, w_ref, c_ref, sn_ref, sp_ref, o_ref, *, rope_lo, rope_hi):
    j = pl.program_id(1)
    acc = jnp.dot(x_ref[...], w_ref[...], preferred_element_type=F32)
    is_rope = jnp.logical_and(j >= rope_lo, j < rope_hi)

    @pl.when(is_rope)
    def _():
        c = c_ref[...]
        sn = sn_ref[...]
        sp = sp_ref[...]
        for h in range(acc.shape[1] // HEAD_DIM):
            xh = acc[:, h * HEAD_DIM:(h + 1) * HEAD_DIM]
            up = pltpu.roll(xh, HEAD_DIM - ROPE_HALF, axis=1)
            dn = pltpu.roll(xh, ROPE_HALF, axis=1)
            o_ref[:, h * HEAD_DIM:(h + 1) * HEAD_DIM] = (xh * c + up * sn + dn * sp).astype(o_ref.dtype)

    @pl.when(jnp.logical_not(is_rope))
    def _():
        o_ref[...] = acc.astype(o_ref.dtype)


def _inproj(xb, w_main, rope_c, rope_sn, rope_sp, seq, rope_cols):
    n, d = xb.shape
    width = w_main.shape[1]
    tm, tn = 1024, 512
    seq_tiles = seq // tm
    kern = functools.partial(_inproj_kernel, rope_lo=rope_cols[0] // tn, rope_hi=rope_cols[1] // tn)
    tab_spec = pl.BlockSpec((tm, LANES), lambda i, j: (i % seq_tiles, 0))
    return pl.pallas_call(
        kern,
        grid=(n // tm, width // tn),
        in_specs=[pl.BlockSpec((tm, d), lambda i, j: (i, 0)),
                  pl.BlockSpec((d, tn), lambda i, j: (0, j)),
                  tab_spec, tab_spec, tab_spec],
        out_specs=pl.BlockSpec((tm, tn), lambda i, j: (i, j)),
        out_shape=jax.ShapeDtypeStruct((n, width), BF16),
        compiler_params=_params(2, 48),
        name="inproj",
    )(xb, w_main, rope_c, rope_sn, rope_sp)


def _split3_f32(a):
    hi = a.astype(BF16).astype(F32)
    r1 = a - hi
    mid = r1.astype(BF16).astype(F32)
    lo = (r1 - mid).astype(BF16).astype(F32)
    return hi, mid, lo


def _gate_kernel(x_ref, w_ref, b_ref, qa_ref, ka_ref, tri_ref, carry_ref, *, seq_tiles):
    i = pl.program_id(0)
    tm = x_ref.shape[0]

    @pl.when(i == 0)
    def _():
        row = lax.broadcasted_iota(I32, (tm, tm), 0)
        col = lax.broadcasted_iota(I32, (tm, tm), 1)
        tri_ref[...] = jnp.where(col <= row, 1.0, 0.0).astype(BF16)

    @pl.when(i % seq_tiles == 0)
    def _():
        carry_ref[...] = jnp.zeros(carry_ref.shape, F32)

    z = jnp.dot(x_ref[...], w_ref[...], preferred_element_type=F32) + b_ref[...]
    log_f = -(jnp.maximum(-z, 0.0) + jnp.log1p(jnp.exp(-jnp.abs(z))))
    tri = tri_ref[...]
    cum = carry_ref[0:1, :]
    for piece in _split3_f32(log_f):
        cum = cum + jnp.dot(tri, piece.astype(BF16), preferred_element_type=F32)
    carry_ref[0:1, :] = cum[tm - 1:tm, :]
    c = cum * (1.0 / ATTN_SCALE)
    lane = lax.broadcasted_iota(I32, (tm, LANES), 1)
    for h in range(qa_ref.shape[1] // LANES):
        hi, mid, lo = _split3_f32(jnp.broadcast_to(c[:, h:h + 1], (tm, LANES)))
        qa = jnp.where(lane == 0, hi, jnp.where(lane == 1, mid, jnp.where(lane == 2, lo,
                       jnp.where(lane < 6, 1.0, 0.0))))
        ka = jnp.where(lane < 3, 1.0, jnp.where(lane == 3, -hi, jnp.where(lane == 4, -mid,
                       jnp.where(lane == 5, -lo, 0.0))))
        qa_ref[:, h * LANES:(h + 1) * LANES] = qa.astype(BF16)
        ka_ref[:, h * LANES:(h + 1) * LANES] = ka.astype(BF16)


def _gate(xb, w_f, b_f, seq):
    n, d = xb.shape
    tm = 1024
    width = FOX_HEADS * LANES
    kern = functools.partial(_gate_kernel, seq_tiles=seq // tm)
    out_spec = pl.BlockSpec((tm, width), lambda i: (i, 0))
    return pl.pallas_call(
        kern,
        grid=(n // tm,),
        in_specs=[pl.BlockSpec((tm, d), lambda i: (i, 0)),
                  pl.BlockSpec((d, LANES), lambda i: (0, 0)),
                  pl.BlockSpec((1, LANES), lambda i: (0, 0))],
        out_specs=[out_spec, out_spec],
        out_shape=[jax.ShapeDtypeStruct((n, width), BF16), jax.ShapeDtypeStruct((n, width), BF16)],
        scratch_shapes=[pltpu.VMEM((tm, tm), BF16), pltpu.VMEM((8, LANES), F32)],
        compiler_params=_params(1, 32),
        name="forget_gate",
    )(xb, w_f, b_f)


EXP2_SCALE = ATTN_SCALE * math.log2(math.e)


def _causal_mask(tq, tk):
    row = lax.broadcasted_iota(I32, (tq, tk), 0)
    col = lax.broadcasted_iota(I32, (tq, tk), 1)
    return col <= row


def _softmax_update(s, v, m_ref, l_ref, a_ref, mask):
    tq, tk = s.shape
    if mask is not None:
        s = jnp.where(mask, s, -jnp.inf)
    m_prev = m_ref[...]
    m_new = jnp.maximum(m_prev, jnp.max(s, axis=1, keepdims=True))
    alpha = jnp.exp2(EXP2_SCALE * (m_prev - m_new))
    p = jnp.exp2(EXP2_SCALE * (s - _lane_tile(m_new, tk // LANES)))
    psum = p[:, 0:LANES]
    for t in range(1, tk // LANES):
        psum = psum + p[:, t * LANES:(t + 1) * LANES]
    l_ref[...] = alpha * l_ref[...] + psum
    a_ref[...] = _lane_tile(alpha, a_ref.shape[1] // LANES) * a_ref[...] + jnp.dot(
        p.astype(BF16), v, preferred_element_type=F32)
    m_ref[...] = m_new


def _init_softmax_state(m_ref, l_ref, a_ref):
    m_ref[...] = jnp.full(m_ref.shape, -jnp.inf, F32)
    l_ref[...] = jnp.zeros(l_ref.shape, F32)
    a_ref[...] = jnp.zeros(a_ref.shape, F32)


def _normalised(a_ref, l_ref):
    l = jnp.sum(l_ref[...], axis=1, keepdims=True)
    return a_ref[...] / l


def _pipelined_causal_sweep(qi, scores, update):
    scores(0, 0)

    def body(t, carry):
        j = 2 * t
        scores(j + 1, 1)
        update(j, 0, False)
        scores(j + 2, 0)
        update(j + 1, 1, False)
        return carry

    lax.fori_loop(0, qi // 2, body, 0)
    j = (qi // 2) * 2

    @pl.when(qi % 2 == 1)
    def _():
        scores(j + 1, 1)
        update(j, 0, False)
        update(j + 1, 1, True)

    @pl.when(qi % 2 == 0)
    def _():
        update(j, 0, True)


def _fox_kernel(q_ref, qa_ref, k_ref, ka_ref, v_ref, o_ref, sa_ref, sb_ref, m_ref, l_ref, acc_ref, *, blk):
    qi = pl.program_id(2)
    _init_softmax_state(m_ref, l_ref, acc_ref)
    q = jnp.concatenate([q_ref[...], qa_ref[...]], axis=1)
    mask = _causal_mask(blk, blk)
    s_refs = (sa_ref, sb_ref)

    def scores(j, buf):
        ks = pl.multiple_of(j * blk, blk)
        kb = jnp.concatenate([k_ref[pl.ds(ks, blk), :], ka_ref[pl.ds(ks, blk), :]], axis=1)
        s_refs[buf][...] = lax.dot_general(q, kb, NT_DIMS, preferred_element_type=F32)

    def update(j, buf, diagonal):
        ks = pl.multiple_of(j * blk, blk)
        _softmax_update(s_refs[buf][...], v_ref[pl.ds(ks, blk), :], m_ref, l_ref, acc_ref,
                        mask if diagonal else None)

    _pipelined_causal_sweep(qi, scores, update)
    o_ref[...] = _normalised(acc_ref, l_ref).astype(o_ref.dtype)


def _fox_attention(proj, qaug, kaug, batch, seq):
    n = proj.shape[0]
    blk = 512
    nq = seq // blk
    kern = functools.partial(_fox_kernel, blk=blk)
    return pl.pallas_call(
        kern,
        grid=(batch, FOX_HEADS, nq),
        in_specs=[pl.BlockSpec((blk, HEAD_DIM), lambda b, h, i: (b * nq + i, h)),
                  pl.BlockSpec((blk, HEAD_DIM), lambda b, h, i: (b * nq + i, h)),
                  pl.BlockSpec((seq, HEAD_DIM), lambda b, h, i: (b, FOX_HEADS + h)),
                  pl.BlockSpec((seq, HEAD_DIM), lambda b, h, i: (b, h)),
                  pl.BlockSpec((seq, HEAD_DIM), lambda b, h, i: (b, 2 * FOX_HEADS + h))],
        out_specs=pl.BlockSpec((blk, HEAD_DIM), lambda b, h, i: (b * nq + i, h)),
        out_shape=jax.ShapeDtypeStruct((n, FOX_HEADS * HEAD_DIM), BF16),
        scratch_shapes=[pltpu.VMEM((blk, blk), F32),
                        pltpu.VMEM((blk, blk), F32),
                        pltpu.VMEM((blk, LANES), F32),
                        pltpu.VMEM((blk, LANES), F32),
                        pltpu.VMEM((blk, HEAD_DIM), F32)],
        compiler_params=_params(3, 48),
        name="fox_attention",
    )(proj, qaug, proj, kaug, proj)


def _diff_kernel(lamv_ref, g_ref, q1_ref, q2_ref, k1_ref, k2_ref, v_ref, o_ref,
                 s1a_ref, s1b_ref, s2a_ref, s2b_ref, m1_ref, l1_ref, a1_ref, m2_ref, l2_ref, a2_ref,
                 *, blk, lambda_init):
    qi = pl.program_id(2)
    _init_softmax_state(m1_ref, l1_ref, a1_ref)
    _init_softmax_state(m2_ref, l2_ref, a2_ref)
    mask = _causal_mask(blk, blk)
    s1_refs = (s1a_ref, s1b_ref)
    s2_refs = (s2a_ref, s2b_ref)

    def scores(j, buf):
        ks = pl.multiple_of(j * blk, blk)
        s1_refs[buf][...] = lax.dot_general(q1_ref[...], k1_ref[pl.ds(ks, blk), :], NT_DIMS,
                                            preferred_element_type=F32)
        s2_refs[buf][...] = lax.dot_general(q2_ref[...], k2_ref[pl.ds(ks, blk), :], NT_DIMS,
                                            preferred_element_type=F32)

    def update(j, buf, diagonal):
        ks = pl.multiple_of(j * blk, blk)
        v = v_ref[pl.ds(ks, blk), :]
        m = mask if diagonal else None
        _softmax_update(s1_refs[buf][...], v, m1_ref, l1_ref, a1_ref, m)
        _softmax_update(s2_refs[buf][...], v, m2_ref, l2_ref, a2_ref, m)

    _pipelined_causal_sweep(qi, scores, update)

    lv = lamv_ref[...]
    t1 = jnp.sum(lv[0:1, :] * lv[1:2, :], axis=1, keepdims=True)
    t2 = jnp.sum(lv[2:3, :] * lv[3:4, :], axis=1, keepdims=True)
    lam = jnp.exp(t1) - jnp.exp(t2) + lambda_init
    o = _normalised(a1_ref, l1_ref) - lam * _normalised(a2_ref, l2_ref)
    ms = jnp.mean(o * o, axis=1, keepdims=True)
    y = o * lax.rsqrt(ms + RMS_EPS) * g_ref[...]
    o_ref[...] = (y * (1.0 - lambda_init)).astype(o_ref.dtype)


def _diff_attention(proj, lam_vecs, subln_g, layer, lambda_init, batch, seq, q_blk, k_blk, v_blk):
    n = proj.shape[0]
    blk = 512
    nq = seq // blk
    dv = 2 * HEAD_DIM
    kern = functools.partial(_diff_kernel, blk=blk, lambda_init=lambda_init)
    sbuf = pltpu.VMEM((blk, blk), F32)
    stat = pltpu.VMEM((blk, LANES), F32)
    accs = pltpu.VMEM((blk, dv), F32)
    return pl.pallas_call(
        kern,
        grid=(batch, DIFF_HEADS, nq),
        in_specs=[pl.BlockSpec((None, 4, HEAD_DIM), lambda b, h, i: (layer, 0, 0)),
                  pl.BlockSpec((None, 1, dv), lambda b, h, i: (layer, 0, 0)),
                  pl.BlockSpec((blk, HEAD_DIM), lambda b, h, i: (b * nq + i, q_blk + 2 * h)),
                  pl.BlockSpec((blk, HEAD_DIM), lambda b, h, i: (b * nq + i, q_blk + 2 * h + 1)),
                  pl.BlockSpec((seq, HEAD_DIM), lambda b, h, i: (b, k_blk + 2 * h)),
                  pl.BlockSpec((seq, HEAD_DIM), lambda b, h, i: (b, k_blk + 2 * h + 1)),
                  pl.BlockSpec((seq, dv), lambda b, h, i: (b, v_blk + h))],
        out_specs=pl.BlockSpec((blk, dv), lambda b, h, i: (b * nq + i, h)),
        out_shape=jax.ShapeDtypeStruct((n, DIFF_HEADS * dv), BF16),
        scratch_shapes=[sbuf, sbuf, sbuf, sbuf, stat, stat, accs, stat, stat, accs],
        compiler_params=_params(3, 48),
        name="diff_attention",
    )(lam_vecs, subln_g, proj, proj, proj, proj, proj)


def _layer_norm_rows(y, g, b):
    mu = jnp.mean(y, axis=1, keepdims=True)
    yc = y - mu
    var = jnp.mean(yc * yc, axis=1, keepdims=True)
    return yc * lax.rsqrt(var + LN_EPS) * g + b


def _outproj_ln_kernel(fa_ref, da_ref, x_ref, wf_ref, wd_ref, g_ref, b_ref, o_ref, *, alpha):
    mix = jnp.dot(fa_ref[...], wf_ref[...], preferred_element_type=F32)
    mix = mix + jnp.dot(da_ref[...], wd_ref[...], preferred_element_type=F32)
    o_ref[...] = _layer_norm_rows(alpha * x_ref[...] + mix, g_ref[...], b_ref[...])


def _outproj_ln(fox_out, diff_out, xf, w_o, ln_g, ln_b, layer, alpha):
    n, d = xf.shape
    half = fox_out.shape[1]
    tm = 256
    kern = functools.partial(_outproj_ln_kernel, alpha=alpha)
    row_spec = pl.BlockSpec((tm, d), lambda i: (i, 0))
    vec_spec = pl.BlockSpec((None, 1, d), lambda i: (layer, 0, 0))
    return pl.pallas_call(
        kern,
        grid=(n // tm,),
        in_specs=[pl.BlockSpec((tm, half), lambda i: (i, 0)),
                  pl.BlockSpec((tm, half), lambda i: (i, 0)),
                  row_spec,
                  pl.BlockSpec((half, d), lambda i: (0, 0)),
                  pl.BlockSpec((half, d), lambda i: (1, 0)),
                  vec_spec, vec_spec],
        out_specs=row_spec,
        out_shape=jax.ShapeDtypeStruct((n, d), F32),
        compiler_params=_params(1, 48),
        name="outproj_ln",
    )(fox_out, diff_out, xf, w_o, w_o, ln_g, ln_b)


def _split_bf16(a):
    hi = a.astype(BF16)
    lo = (a - hi.astype(F32)).astype(BF16)
    return hi, lo


def _router_kernel(x_ref, w_ref, b_ref, gate_ref, eid_ref):
    x_hi, x_lo = _split_bf16(x_ref[...])
    w_hi, w_lo = _split_bf16(w_ref[...])
    dg = functools.partial(lax.dot_general, dimension_numbers=NT_DIMS, preferred_element_type=F32)
    lg = dg(w_hi, x_hi) + dg(w_hi, x_lo) + dg(w_lo, x_hi) + b_ref[...]

    def first_index(vals, top):
        idx = jnp.full(top.shape, len(vals) - 1, I32)
        for j in range(len(vals) - 2, -1, -1):
            idx = jnp.where(vals[j] == top, j, idx)
        return idx

    g = [lg[i:i + 1, :] for i in range(N_GROUPS)]
    gmax = functools.reduce(jnp.maximum, g)
    gsum = functools.reduce(lambda a, b: a + b, [jnp.exp(gi - gmax) for gi in g])
    g_w = 1.0 / gsum
    sel = first_index(g, gmax)

    es = []
    for j in range(EXPERTS_PER_GROUP):
        rows = [lg[N_GROUPS + gi * EXPERTS_PER_GROUP + j:N_GROUPS + gi * EXPERTS_PER_GROUP + j + 1, :]
                for gi in range(N_GROUPS)]
        e = rows[N_GROUPS - 1]
        for gi in range(N_GROUPS - 2, -1, -1):
            e = jnp.where(sel == gi, rows[gi], e)
        es.append(e)
    emax = functools.reduce(jnp.maximum, es)
    ex = [jnp.exp(e - emax) for e in es]
    esum = functools.reduce(lambda a, b: a + b, ex)
    prob = [e / esum for e in ex]
    v1 = functools.reduce(jnp.maximum, prob)
    i1 = first_index(prob, v1)
    rest = [jnp.where(i1 == j, -1.0, prob[j]) for j in range(EXPERTS_PER_GROUP)]
    v2 = functools.reduce(jnp.maximum, rest)
    i2 = first_index(rest, v2)
    den = v1 + v2
    gate_ref[...] = jnp.zeros(gate_ref.shape, F32)
    eid_ref[...] = jnp.zeros(eid_ref.shape, I32)
    gate_ref[0:1, :] = g_w * (v1 / den)
    gate_ref[1:2, :] = g_w * (v2 / den)
    eid_ref[0:1, :] = sel * EXPERTS_PER_GROUP + i1
    eid_ref[1:2, :] = sel * EXPERTS_PER_GROUP + i2


def _router(xf, w_rt, b_r):
    n, d = xf.shape
    rows = w_rt.shape[0]
    tm = 512
    return pl.pallas_call(
        _router_kernel,
        grid=(n // tm,),
        in_specs=[pl.BlockSpec((tm, d), lambda i: (i, 0)),
                  pl.BlockSpec((rows, d), lambda i: (0, 0)),
                  pl.BlockSpec((rows, 1), lambda i: (0, 0))],
        out_specs=[pl.BlockSpec((8, tm), lambda i: (0, i)),
                   pl.BlockSpec((8, tm), lambda i: (0, i))],
        out_shape=[jax.ShapeDtypeStruct((8, n), F32), jax.ShapeDtypeStruct((8, n), I32)],
        compiler_params=_params(1, 32),
        name="router",
    )(xf, w_rt, b_r)


def _dispatch_kernel(pos_ref, x_ref, xs_ref, sem, *, tm):
    def row_copy(r, p):
        return pltpu.make_async_copy(x_ref.at[pl.ds(r, 1), :], xs_ref.at[pl.ds(p, 1), :], sem)

    def issue(r, carry):
        for k in range(TOP_K):
            row_copy(r, pos_ref[k, r]).start()
        return carry

    lax.fori_loop(0, tm, issue, 0, unroll=8)
    for k in range(TOP_K):
        pltpu.make_async_copy(x_ref, xs_ref.at[pl.ds(0, tm), :], sem).wait()


def _dispatch(pos_tiles, xf):
    n, d = xf.shape
    tm = pos_tiles.shape[2]
    kern = functools.partial(_dispatch_kernel, tm=tm)
    return pl.pallas_call(
        kern,
        grid=(n // tm,),
        in_specs=[pl.BlockSpec((None, TOP_K, tm), lambda i: (i, 0, 0), memory_space=pltpu.SMEM),
                  pl.BlockSpec((tm, d), lambda i: (i, 0))],
        out_specs=pl.BlockSpec(memory_space=pl.ANY),
        out_shape=jax.ShapeDtypeStruct((TOP_K * n, d), F32),
        scratch_shapes=[pltpu.SemaphoreType.DMA(())],
        compiler_params=_params(1, 32),
        name="dispatch",
    )(pos_tiles, xf)


def _expert_kernel(chunk_ref, expert_ref, lo_ref, hi_ref, first_ref, nitems_ref,
                   xs_ref, wg_ref, wu_ref, wd_ref, ys_ref):
    i = pl.program_id(0)

    @pl.when(i < nitems_ref[0])
    def _():
        xb = xs_ref[...].astype(BF16)
        g = jnp.dot(xb, wg_ref[...].astype(BF16), preferred_element_type=F32)
        u = jnp.dot(xb, wu_ref[...].astype(BF16), preferred_element_type=F32)
        hid = (g * (1.0 / (1.0 + jnp.exp(-g)))) * u
        y = jnp.dot(hid.astype(BF16), wd_ref[...].astype(BF16), preferred_element_type=F32)
        row = lax.broadcasted_iota(I32, y.shape, 0)
        mine = jnp.logical_and(row >= lo_ref[i], row < hi_ref[i])

        @pl.when(first_ref[i] == 1)
        def _():
            ys_ref[...] = jnp.where(mine, y, 0.0)

        @pl.when(first_ref[i] == 0)
        def _():
            ys_ref[...] = jnp.where(mine, y, ys_ref[...])


def _experts(meta, xs, w_gate, w_up, w_down, layer):
    rows, d = xs.shape
    de = w_gate.shape[3]
    c = EXPERT_CHUNK
    n_items = meta[0].shape[0]
    grid_spec = pltpu.PrefetchScalarGridSpec(
        num_scalar_prefetch=6,
        grid=(n_items,),
        in_specs=[pl.BlockSpec((c, d), lambda i, ch, ex, lo, hi, fi, ni: (ch[i], 0)),
                  pl.BlockSpec((None, None, d, de), lambda i, ch, ex, lo, hi, fi, ni: (layer, ex[i], 0, 0)),
                  pl.BlockSpec((None, None, d, de), lambda i, ch, ex, lo, hi, fi, ni: (layer, ex[i], 0, 0)),
                  pl.BlockSpec((None, None, de, d), lambda i, ch, ex, lo, hi, fi, ni: (layer, ex[i], 0, 0))],
        out_specs=pl.BlockSpec((c, d), lambda i, ch, ex, lo, hi, fi, ni: (ch[i], 0)),
    )
    return pl.pallas_call(
        _expert_kernel,
        grid_spec=grid_spec,
        out_shape=jax.ShapeDtypeStruct((rows, d), F32),
        compiler_params=_params(1, 56),
        name="experts",
    )(*meta, xs, w_gate, w_up, w_down)


def _combine_ln_kernel(pos_ref, x_ref, g0_ref, g1_ref, gam_ref, bet_ref, ys_ref, o_ref, ob_ref,
                       y0_ref, y1_ref, sem, *, tm, alpha):
    bufs = (y0_ref, y1_ref)

    def row_copy(k, r, p):
        return pltpu.make_async_copy(ys_ref.at[pl.ds(p, 1), :], bufs[k].at[pl.ds(r, 1), :], sem)

    def issue(r, carry):
        for k in range(TOP_K):
            row_copy(k, r, pos_ref[k, r]).start()
        return carry

    lax.fori_loop(0, tm, issue, 0, unroll=8)
    for k in range(TOP_K):
        pltpu.make_async_copy(ys_ref.at[pl.ds(0, tm), :], bufs[k], sem).wait()

    reps = x_ref.shape[1] // LANES
    ffn = _lane_tile(g0_ref[...], reps, axis=1) * y0_ref[...]
    ffn = ffn + _lane_tile(g1_ref[...], reps, axis=1) * y1_ref[...]
    out = _layer_norm_rows(alpha * x_ref[...] + ffn, gam_ref[...], bet_ref[...])
    o_ref[...] = out
    ob_ref[...] = out.astype(BF16)


def _combine_ln(pos_tiles, xf, gate0, gate1, ys, ln_g, ln_b, layer, alpha):
    n, d = xf.shape
    tm = pos_tiles.shape[2]
    kern = functools.partial(_combine_ln_kernel, tm=tm, alpha=alpha)
    row_spec = pl.BlockSpec((tm, d), lambda i: (i, 0))
    gate_spec = pl.BlockSpec((tm, LANES), lambda i: (i, 0))
    vec_spec = pl.BlockSpec((None, 1, d), lambda i: (layer, 0, 0))
    return pl.pallas_call(
        kern,
        grid=(n // tm,),
        in_specs=[pl.BlockSpec((None, TOP_K, tm), lambda i: (i, 0, 0), memory_space=pltpu.SMEM),
                  row_spec, gate_spec, gate_spec, vec_spec, vec_spec,
                  pl.BlockSpec(memory_space=pl.ANY)],
        out_specs=[row_spec, row_spec],
        out_shape=[jax.ShapeDtypeStruct((n, d), F32), jax.ShapeDtypeStruct((n, d), BF16)],
        scratch_shapes=[pltpu.VMEM((tm, d), F32), pltpu.VMEM((tm, d), F32),
                        pltpu.SemaphoreType.DMA(())],
        compiler_params=_params(1, 40),
        name="combine_ln",
    )(pos_tiles, xf, gate0, gate1, ln_g, ln_b, ys)


def _routing_tables(eid, n_items):
    flat = eid.reshape(-1)
    c = EXPERT_CHUNK
    onehot = (flat[:, None] == jnp.arange(N_EXPERTS, dtype=I32)[None, :]).astype(I32)
    csum = jnp.cumsum(onehot, axis=0)
    counts = csum[-1]
    ends = jnp.cumsum(counts)
    starts = ends - counts
    rank = jnp.sum(csum * onehot, axis=1) - 1
    pos = jnp.sum(onehot * starts[None, :], axis=1) + rank

    first_chunk = starts // c
    last_chunk = (ends - 1) // c
    items_per = jnp.where(counts > 0, last_chunk - first_chunk + 1, 0)
    item_ends = jnp.cumsum(items_per)
    total = item_ends[-1]
    it = jnp.minimum(jnp.arange(n_items, dtype=I32), total - 1)
    ex = jnp.sum((it[:, None] >= item_ends[None, :]).astype(I32), axis=1)
    ex_oh = (ex[:, None] == jnp.arange(N_EXPERTS, dtype=I32)[None, :]).astype(I32)
    pick = lambda v: jnp.sum(ex_oh * v[None, :], axis=1)
    chunk = pick(first_chunk) + it - (pick(item_ends) - pick(items_per))
    lo = jnp.maximum(pick(starts), chunk * c) - chunk * c
    hi = jnp.minimum(pick(ends), (chunk + 1) * c) - chunk * c
    prev_chunk = jnp.concatenate([jnp.full((1,), -1, I32), chunk[:-1]])
    first = (chunk != prev_chunk).astype(I32)
    meta = (chunk.astype(I32), ex.astype(I32), lo.astype(I32), hi.astype(I32), first,
            total.reshape(1).astype(I32))
    return pos.reshape(eid.shape).astype(I32), meta


def _rope_tables(seq):
    pos = jnp.arange(seq, dtype=F32)
    inv_freq = ROPE_THETA ** (-jnp.arange(0, ROPE_DIM, 2, dtype=F32) / ROPE_DIM)
    ang = pos[:, None] * inv_freq[None, :]
    cos, sin = jnp.cos(ang), jnp.sin(ang)
    ones = jnp.ones((seq, HEAD_DIM - ROPE_DIM), F32)
    zeros_half = jnp.zeros((seq, ROPE_HALF), F32)
    zeros_rest = jnp.zeros((seq, HEAD_DIM - ROPE_DIM), F32)
    c_tab = jnp.concatenate([cos, cos, ones], axis=1)
    sn_tab = jnp.concatenate([-sin, zeros_half, zeros_rest], axis=1)
    sp_tab = jnp.concatenate([zeros_half, sin, zeros_rest], axis=1)
    return c_tab, sn_tab, sp_tab


def kernel(x, w_in, b_f, diff_lambda, diff_subln_g, w_o, ln1_g, ln1_b, w_router_group,
           b_router_group, w_router_expert, b_router_expert, w_gate, w_up, w_down, ln2_g, ln2_b):
    batch, seq, d = x.shape
    depth = w_in.shape[0]
    n = batch * seq
    alpha = (2.0 * depth) ** 0.25
    fox_w = FOX_HEADS * HEAD_DIM
    diff_qk_w = 2 * DIFF_HEADS * HEAD_DIM
    gate_lo = 3 * fox_w
    gate_hi = gate_lo + FOX_HEADS
    q_blk = 3 * FOX_HEADS
    k_blk = q_blk + 2 * DIFF_HEADS
    v_blk = (k_blk + 2 * DIFF_HEADS) // 2
    rope_cols = (gate_lo, gate_lo + 2 * diff_qk_w)

    rope_c, rope_sn, rope_sp = _rope_tables(seq)
    xf = x.reshape(n, d)
    xb = xf.astype(BF16)
    n_items = (TOP_K * n) // EXPERT_CHUNK + N_EXPERTS
    tile = 256

    for layer in range(depth):
        lambda_init = 0.8 - 0.6 * math.exp(-0.3 * layer)
        wl = w_in[layer]
        w_main = jnp.concatenate([wl[:, :gate_lo], wl[:, gate_hi:]], axis=1).astype(BF16)
        w_f = jnp.pad(wl[:, gate_lo:gate_hi], ((0, 0), (0, LANES - FOX_HEADS))).astype(BF16)
        bias_f = jnp.pad(b_f[layer], (0, LANES - FOX_HEADS)).reshape(1, LANES)

        proj = _inproj(xb, w_main, rope_c, rope_sn, rope_sp, seq, rope_cols)
        qaug, kaug = _gate(xb, w_f, bias_f, seq)
        fox_out = _fox_attention(proj, qaug, kaug, batch, seq)
        diff_out = _diff_attention(proj, diff_lambda, diff_subln_g.reshape(depth, 1, -1), layer,
                                   lambda_init, batch, seq, q_blk, k_blk, v_blk)
        x1 = _outproj_ln(fox_out, diff_out, xf, w_o[layer].astype(BF16),
                         ln1_g.reshape(depth, 1, d), ln1_b.reshape(depth, 1, d), layer, alpha)

        w_rt = jnp.concatenate([w_router_group[layer], w_router_expert[layer]], axis=1).T
        w_rt = jnp.pad(w_rt, ((0, 4), (0, 0)))
        b_r = jnp.pad(jnp.concatenate([b_router_group[layer], b_router_expert[layer]]), (0, 4)).reshape(-1, 1)
        gate8, eid8 = _router(x1, w_rt, b_r)
        pos, meta = _routing_tables(eid8[:TOP_K], n_items)
        pos_tiles = pos.reshape(TOP_K, n // tile, tile).transpose(1, 0, 2)
        xs = _dispatch(pos_tiles, x1)
        ys = _experts(meta, xs, w_gate, w_up, w_down, layer)
        gate0 = jnp.broadcast_to(gate8[0][:, None], (n, LANES))
        gate1 = jnp.broadcast_to(gate8[1][:, None], (n, LANES))
        xf, xb = _combine_ln(pos_tiles, x1, gate0, gate1, ys, ln2_g.reshape(depth, 1, d),
                             ln2_b.reshape(depth, 1, d), layer, alpha)
    return xf.reshape(batch, seq, d)
```

```python
import functools
import math

import jax
import jax.numpy as jnp
from jax import lax
from jax.experimental import pallas as pl
from jax.experimental.pallas import tpu as pltpu

F32 = jnp.float32
BF16 = jnp.bfloat16
I32 = jnp.int32

HEAD_DIM = 128
LANES = 128
FOX_HEADS = 8
DIFF_HEADS = 4
ROPE_DIM = HEAD_DIM // 4
ROPE_HALF = ROPE_DIM // 2
ROPE_THETA = 500000.0
N_GROUPS = 4
EXPERTS_PER_GROUP = 8
N_EXPERTS = N_GROUPS * EXPERTS_PER_GROUP
TOP_K = 2
LN_EPS = 1e-5
RMS_EPS = 1e-5
ATTN_SCALE = HEAD_DIM ** -0.5
EXPERT_CHUNK = 256
MIB = 1024 * 1024

NT_DIMS = (((1,), (1,)), ((), ()))


def _lane_tile(a, reps, axis=1):
    assert axis == 1
    return a if reps == 1 else jnp.concatenate([a] * reps, axis=1)


def _params(n_axes, vmem_mib):
    return pltpu.CompilerParams(
        dimension_semantics=("arbitrary",) * n_axes,
        vmem_limit_bytes=vmem_mib * MIB)


def _inproj_kernel(x_ref, w_ref, c_ref, sn_ref, sp_ref, o_ref, *, rope_lo, rope_hi):
    j = pl.program_id(1)
    is_rope = jnp.logical_and(j >= rope_lo, j < rope_hi)

    @pl.when(is_rope)
    def _():
        acc = jnp.dot(x_ref[...], w_ref[...], preferred_element_type=F32)
        c = c_ref[...]
        sn = sn_ref[...]
        sp = sp_ref[...]
        for h in range(acc.shape[1] // HEAD_DIM):
            xh = acc[:, h * HEAD_DIM:(h + 1) * HEAD_DIM]
            up = pltpu.roll(xh, HEAD_DIM - ROPE_HALF, axis=1)
            dn = pltpu.roll(xh, ROPE_HALF, axis=1)
            o_ref[:, h * HEAD_DIM:(h + 1) * HEAD_DIM] = (xh * c + up * sn + dn * sp).astype(o_ref.dtype)

    @pl.when(jnp.logical_not(is_rope))
    def _():
        o_ref[...] = jnp.dot(x_ref[...], w_ref[...], preferred_element_type=F32).astype(o_ref.dtype)


def _inproj(xb, w_main, rope_c, rope_sn, rope_sp, seq, rope_cols):
    n, d = xb.shape
    width = w_main.shape[1]
    tm, tn = 1024, 512
    seq_tiles = seq // tm
    kern = functools.partial(_inproj_kernel, rope_lo=rope_cols[0] // tn, rope_hi=rope_cols[1] // tn)
    tab_spec = pl.BlockSpec((tm, LANES), lambda i, j: (i % seq_tiles, 0))
    return pl.pallas_call(
        kern,
        grid=(n // tm, width // tn),
        in_specs=[pl.BlockSpec((tm, d), lambda i, j: (i, 0)),
                  pl.BlockSpec((d, tn), lambda i, j: (0, j)),
                  tab_spec, tab_spec, tab_spec],
        out_specs=pl.BlockSpec((tm, tn), lambda i, j: (i, j)),
        out_shape=jax.ShapeDtypeStruct((n, width), BF16),
        compiler_params=_params(2, 48),
        name="inproj",
    )(xb, w_main, rope_c, rope_sn, rope_sp)


def _split3_f32(a):
    hi = a.astype(BF16).astype(F32)
    r1 = a - hi
    mid = r1.astype(BF16).astype(F32)
    lo = (r1 - mid).astype(BF16).astype(F32)
    return hi, mid, lo


def _gate_kernel(x_ref, w_ref, b_ref, eq_ref, ek_ref, oq_ref, ok_ref, qa_ref, ka_ref, tri_ref, carry_ref,
                 *, seq_tiles):
    i = pl.program_id(0)
    tm = x_ref.shape[0]

    @pl.when(i == 0)
    def _():
        row = lax.broadcasted_iota(I32, (tm, tm), 0)
        col = lax.broadcasted_iota(I32, (tm, tm), 1)
        tri_ref[...] = jnp.where(col <= row, 1.0, 0.0).astype(BF16)

    @pl.when(i % seq_tiles == 0)
    def _():
        carry_ref[...] = jnp.zeros(carry_ref.shape, F32)

    z = jnp.dot(x_ref[...], w_ref[...], preferred_element_type=F32) + b_ref[...]
    log_f = -(jnp.maximum(-z, 0.0) + jnp.log1p(jnp.exp(-jnp.abs(z))))
    tri = tri_ref[...]
    cum = carry_ref[0:1, :]
    for piece in _split3_f32(log_f):
        cum = cum + jnp.dot(tri, piece.astype(BF16), preferred_element_type=F32)
    carry_ref[0:1, :] = cum[tm - 1:tm, :]
    pieces = jnp.concatenate([p.astype(BF16) for p in _split3_f32(cum * (1.0 / ATTN_SCALE))], axis=1)
    qa_ref[...] = (jnp.dot(pieces, eq_ref[...], preferred_element_type=F32) + oq_ref[...]).astype(BF16)
    ka_ref[...] = (ok_ref[...] - jnp.dot(pieces, ek_ref[...], preferred_element_type=F32)).astype(BF16)


def _placement_tables():
    width = FOX_HEADS * LANES
    rows = jnp.arange(3 * LANES, dtype=I32)[:, None]
    cols = jnp.arange(width, dtype=I32)[None, :]
    piece, head = rows // LANES, rows % LANES
    e_q = ((cols == head * LANES + piece) & (head < FOX_HEADS)).astype(BF16)
    e_k = ((cols == head * LANES + 3 + piece) & (head < FOX_HEADS)).astype(BF16)
    lane = cols % LANES
    ones_q = ((lane >= 3) & (lane < 6)).astype(F32)
    ones_k = (lane < 3).astype(F32)
    return e_q, e_k, ones_q, ones_k


def _gate(xb, w_f, b_f, seq):
    n, d = xb.shape
    tm = 1024
    width = FOX_HEADS * LANES
    kern = functools.partial(_gate_kernel, seq_tiles=seq // tm)
    out_spec = pl.BlockSpec((tm, width), lambda i: (i, 0))
    const = lambda shape: pl.BlockSpec(shape, lambda i: (0, 0))
    return pl.pallas_call(
        kern,
        grid=(n // tm,),
        in_specs=[pl.BlockSpec((tm, d), lambda i: (i, 0)), const((d, LANES)), const((1, LANES)),
                  const((3 * LANES, width)), const((3 * LANES, width)), const((1, width)), const((1, width))],
        out_specs=[out_spec, out_spec],
        out_shape=[jax.ShapeDtypeStruct((n, width), BF16), jax.ShapeDtypeStruct((n, width), BF16)],
        scratch_shapes=[pltpu.VMEM((tm, tm), BF16), pltpu.VMEM((8, LANES), F32)],
        compiler_params=_params(1, 40),
        name="forget_gate",
    )(xb, w_f, b_f, *_placement_tables())


EXP2_SCALE = ATTN_SCALE * math.log2(math.e)


def _causal_mask(tq, tk):
    row = lax.broadcasted_iota(I32, (tq, tk), 0)
    col = lax.broadcasted_iota(I32, (tq, tk), 1)
    return col <= row


def _softmax_update(s, v, m_ref, l_ref, a_ref, mask):
    tq, tk = s.shape
    if mask is not None:
        s = jnp.where(mask, s, -jnp.inf)
    m_prev = m_ref[...]
    m_new = jnp.maximum(m_prev, jnp.max(s, axis=1, keepdims=True))
    alpha = jnp.exp2(EXP2_SCALE * (m_prev - m_new))
    p = jnp.exp2(EXP2_SCALE * (s - _lane_tile(m_new, tk // LANES)))
    psum = p[:, 0:LANES]
    for t in range(1, tk // LANES):
        psum = psum + p[:, t * LANES:(t + 1) * LANES]
    l_ref[...] = alpha * l_ref[...] + psum
    a_ref[...] = _lane_tile(alpha, a_ref.shape[1] // LANES) * a_ref[...] + jnp.dot(
        p.astype(BF16), v, preferred_element_type=F32)
    m_ref[...] = m_new


def _init_softmax_state(m_ref, l_ref, a_ref):
    m_ref[...] = jnp.full(m_ref.shape, -jnp.inf, F32)
    l_ref[...] = jnp.zeros(l_ref.shape, F32)
    a_ref[...] = jnp.zeros(a_ref.shape, F32)


def _normalised(a_ref, l_ref):
    l = jnp.sum(l_ref[...], axis=1, keepdims=True)
    return a_ref[...] / l


def _pipelined_causal_sweep(qi, scores, update):
    scores(0, 0)

    def body(t, carry):
        j = 2 * t
        scores(j + 1, 1)
        update(j, 0, False)
        scores(j + 2, 0)
        update(j + 1, 1, False)
        return carry

    lax.fori_loop(0, qi // 2, body, 0)
    j = (qi // 2) * 2

    @pl.when(qi % 2 == 1)
    def _():
        scores(j + 1, 1)
        update(j, 0, False)
        update(j + 1, 1, True)

    @pl.when(qi % 2 == 0)
    def _():
        update(j, 0, True)


def _fox_kernel(q_ref, qa_ref, k_ref, ka_ref, v_ref, o_ref, sa_ref, sb_ref, m_ref, l_ref, acc_ref, *, blk):
    qi = pl.program_id(2)
    _init_softmax_state(m_ref, l_ref, acc_ref)
    q = jnp.concatenate([q_ref[...], qa_ref[...]], axis=1)
    mask = _causal_mask(blk, blk)
    s_refs = (sa_ref, sb_ref)

    def scores(j, buf):
        ks = pl.multiple_of(j * blk, blk)
        kb = jnp.concatenate([k_ref[pl.ds(ks, blk), :], ka_ref[pl.ds(ks, blk), :]], axis=1)
        s_refs[buf][...] = lax.dot_general(q, kb, NT_DIMS, preferred_element_type=F32)

    def update(j, buf, diagonal):
        ks = pl.multiple_of(j * blk, blk)
        _softmax_update(s_refs[buf][...], v_ref[pl.ds(ks, blk), :], m_ref, l_ref, acc_ref,
                        mask if diagonal else None)

    _pipelined_causal_sweep(qi, scores, update)
    o_ref[...] = _normalised(acc_ref, l_ref).astype(o_ref.dtype)


def _fox_attention(proj, qaug, kaug, batch, seq):
    n = proj.shape[0]
    blk = 512
    nq = seq // blk
    kern = functools.partial(_fox_kernel, blk=blk)
    return pl.pallas_call(
        kern,
        grid=(batch, FOX_HEADS, nq),
        in_specs=[pl.BlockSpec((blk, HEAD_DIM), lambda b, h, i: (b * nq + i, h)),
                  pl.BlockSpec((blk, HEAD_DIM), lambda b, h, i: (b * nq + i, h)),
                  pl.BlockSpec((seq, HEAD_DIM), lambda b, h, i: (b, FOX_HEADS + h)),
                  pl.BlockSpec((seq, HEAD_DIM), lambda b, h, i: (b, h)),
                  pl.BlockSpec((seq, HEAD_DIM), lambda b, h, i: (b, 2 * FOX_HEADS + h))],
        out_specs=pl.BlockSpec((blk, HEAD_DIM), lambda b, h, i: (b * nq + i, h)),
        out_shape=jax.ShapeDtypeStruct((n, FOX_HEADS * HEAD_DIM), BF16),
        scratch_shapes=[pltpu.VMEM((blk, blk), F32),
                        pltpu.VMEM((blk, blk), F32),
                        pltpu.VMEM((blk, LANES), F32),
                        pltpu.VMEM((blk, LANES), F32),
                        pltpu.VMEM((blk, HEAD_DIM), F32)],
        compiler_params=_params(3, 48),
        name="fox_attention",
    )(proj, qaug, proj, kaug, proj)


def _diff_kernel(lamv_ref, g_ref, q1_ref, q2_ref, k1_ref, k2_ref, v_ref, o_ref,
                 s1a_ref, s1b_ref, s2a_ref, s2b_ref, m1_ref, l1_ref, a1_ref, m2_ref, l2_ref, a2_ref,
                 *, blk, lambda_init):
    qi = pl.program_id(2)
    _init_softmax_state(m1_ref, l1_ref, a1_ref)
    _init_softmax_state(m2_ref, l2_ref, a2_ref)
    mask = _causal_mask(blk, blk)
    s1_refs = (s1a_ref, s1b_ref)
    s2_refs = (s2a_ref, s2b_ref)

    def scores(j, buf):
        ks = pl.multiple_of(j * blk, blk)
        s1_refs[buf][...] = lax.dot_general(q1_ref[...], k1_ref[pl.ds(ks, blk), :], NT_DIMS,
                                            preferred_element_type=F32)
        s2_refs[buf][...] = lax.dot_general(q2_ref[...], k2_ref[pl.ds(ks, blk), :], NT_DIMS,
                                            preferred_element_type=F32)

    def update(j, buf, diagonal):
        ks = pl.multiple_of(j * blk, blk)
        v = v_ref[pl.ds(ks, blk), :]
        m = mask if diagonal else None
        _softmax_update(s1_refs[buf][...], v, m1_ref, l1_ref, a1_ref, m)
        _softmax_update(s2_refs[buf][...], v, m2_ref, l2_ref, a2_ref, m)

    _pipelined_causal_sweep(qi, scores, update)

    lv = lamv_ref[...]
    t1 = jnp.sum(lv[0:1, :] * lv[1:2, :], axis=1, keepdims=True)
    t2 = jnp.sum(lv[2:3, :] * lv[3:4, :], axis=1, keepdims=True)
    lam = jnp.exp(t1) - jnp.exp(t2) + lambda_init
    o = _normalised(a1_ref, l1_ref) - lam * _normalised(a2_ref, l2_ref)
    ms = jnp.mean(o * o, axis=1, keepdims=True)
    y = o * lax.rsqrt(ms + RMS_EPS) * g_ref[...]
    o_ref[...] = (y * (1.0 - lambda_init)).astype(o_ref.dtype)


def _diff_attention(proj, lam_vecs, subln_g, layer, lambda_init, batch, seq, q_blk, k_blk, v_blk):
    n = proj.shape[0]
    blk = 512
    nq = seq // blk
    dv = 2 * HEAD_DIM
    kern = functools.partial(_diff_kernel, blk=blk, lambda_init=lambda_init)
    sbuf = pltpu.VMEM((blk, blk), F32)
    stat = pltpu.VMEM((blk, LANES), F32)
    accs = pltpu.VMEM((blk, dv), F32)
    return pl.pallas_call(
        kern,
        grid=(batch, DIFF_HEADS, nq),
        in_specs=[pl.BlockSpec((None, 4, HEAD_DIM), lambda b, h, i: (layer, 0, 0)),
                  pl.BlockSpec((None, 1, dv), lambda b, h, i: (layer, 0, 0)),
                  pl.BlockSpec((blk, HEAD_DIM), lambda b, h, i: (b * nq + i, q_blk + 2 * h)),
                  pl.BlockSpec((blk, HEAD_DIM), lambda b, h, i: (b * nq + i, q_blk + 2 * h + 1)),
                  pl.BlockSpec((seq, HEAD_DIM), lambda b, h, i: (b, k_blk + 2 * h)),
                  pl.BlockSpec((seq, HEAD_DIM), lambda b, h, i: (b, k_blk + 2 * h + 1)),
                  pl.BlockSpec((seq, dv), lambda b, h, i: (b, v_blk + h))],
        out_specs=pl.BlockSpec((blk, dv), lambda b, h, i: (b * nq + i, h)),
        out_shape=jax.ShapeDtypeStruct((n, DIFF_HEADS * dv), BF16),
        scratch_shapes=[sbuf, sbuf, sbuf, sbuf, stat, stat, accs, stat, stat, accs],
        compiler_params=_params(3, 48),
        name="diff_attention",
    )(lam_vecs, subln_g, proj, proj, proj, proj, proj)


def _layer_norm_rows(y, g, b):
    mu = jnp.mean(y, axis=1, keepdims=True)
    yc = y - mu
    var = jnp.mean(yc * yc, axis=1, keepdims=True)
    return yc * lax.rsqrt(var + LN_EPS) * g + b


def _outproj_ln_kernel(fa_ref, da_ref, x_ref, wf_ref, wd_ref, g_ref, b_ref, o_ref, *, alpha):
    mix = jnp.dot(fa_ref[...], wf_ref[...], preferred_element_type=F32)
    mix = mix + jnp.dot(da_ref[...], wd_ref[...], preferred_element_type=F32)
    o_ref[...] = _layer_norm_rows(alpha * x_ref[...] + mix, g_ref[...], b_ref[...])


def _outproj_ln(fox_out, diff_out, xf, w_o, ln_g, ln_b, layer, alpha):
    n, d = xf.shape
    half = fox_out.shape[1]
    tm = 256
    kern = functools.partial(_outproj_ln_kernel, alpha=alpha)
    row_spec = pl.BlockSpec((tm, d), lambda i: (i, 0))
    vec_spec = pl.BlockSpec((None, 1, d), lambda i: (layer, 0, 0))
    return pl.pallas_call(
        kern,
        grid=(n // tm,),
        in_specs=[pl.BlockSpec((tm, half), lambda i: (i, 0)),
                  pl.BlockSpec((tm, half), lambda i: (i, 0)),
                  row_spec,
                  pl.BlockSpec((half, d), lambda i: (0, 0)),
                  pl.BlockSpec((half, d), lambda i: (1, 0)),
                  vec_spec, vec_spec],
        out_specs=row_spec,
        out_shape=jax.ShapeDtypeStruct((n, d), F32),
        compiler_params=_params(1, 48),
        name="outproj_ln",
    )(fox_out, diff_out, xf, w_o, w_o, ln_g, ln_b)


def _split_bf16(a):
    hi = a.astype(BF16)
    lo = (a - hi.astype(F32)).astype(BF16)
    return hi, lo


def _router_kernel(x_ref, w_ref, b_ref, gate_ref, eid_ref):
    x_hi, x_lo = _split_bf16(x_ref[...])
    w_hi, w_lo = _split_bf16(w_ref[...])
    dg = functools.partial(lax.dot_general, dimension_numbers=NT_DIMS, preferred_element_type=F32)
    lg = dg(w_hi, x_hi) + dg(w_hi, x_lo) + dg(w_lo, x_hi) + b_ref[...]

    def first_index(vals, top):
        idx = jnp.full(top.shape, len(vals) - 1, I32)
        for j in range(len(vals) - 2, -1, -1):
            idx = jnp.where(vals[j] == top, j, idx)
        return idx

    g = [lg[i:i + 1, :] for i in range(N_GROUPS)]
    gmax = functools.reduce(jnp.maximum, g)
    gsum = functools.reduce(lambda a, b: a + b, [jnp.exp(gi - gmax) for gi in g])
    g_w = 1.0 / gsum
    sel = first_index(g, gmax)

    es = []
    for j in range(EXPERTS_PER_GROUP):
        rows = [lg[N_GROUPS + gi * EXPERTS_PER_GROUP + j:N_GROUPS + gi * EXPERTS_PER_GROUP + j + 1, :]
                for gi in range(N_GROUPS)]
        e = rows[N_GROUPS - 1]
        for gi in range(N_GROUPS - 2, -1, -1):
            e = jnp.where(sel == gi, rows[gi], e)
        es.append(e)
    emax = functools.reduce(jnp.maximum, es)
    ex = [jnp.exp(e - emax) for e in es]
    esum = functools.reduce(lambda a, b: a + b, ex)
    prob = [e / esum for e in ex]
    v1 = functools.reduce(jnp.maximum, prob)
    i1 = first_index(prob, v1)
    rest = [jnp.where(i1 == j, -1.0, prob[j]) for j in range(EXPERTS_PER_GROUP)]
    v2 = functools.reduce(jnp.maximum, rest)
    i2 = first_index(rest, v2)
    den = v1 + v2
    gate_ref[...] = jnp.zeros(gate_ref.shape, F32)
    eid_ref[...] = jnp.zeros(eid_ref.shape, I32)
    gate_ref[0:1, :] = g_w * (v1 / den)
    gate_ref[1:2, :] = g_w * (v2 / den)
    eid_ref[0:1, :] = sel * EXPERTS_PER_GROUP + i1
    eid_ref[1:2, :] = sel * EXPERTS_PER_GROUP + i2


def _router(xf, w_rt, b_r):
    n, d = xf.shape
    rows = w_rt.shape[0]
    tm = 512
    return pl.pallas_call(
        _router_kernel,
        grid=(n // tm,),
        in_specs=[pl.BlockSpec((tm, d), lambda i: (i, 0)),
                  pl.BlockSpec((rows, d), lambda i: (0, 0)),
                  pl.BlockSpec((rows, 1), lambda i: (0, 0))],
        out_specs=[pl.BlockSpec((8, tm), lambda i: (0, i)),
                   pl.BlockSpec((8, tm), lambda i: (0, i))],
        out_shape=[jax.ShapeDtypeStruct((8, n), F32), jax.ShapeDtypeStruct((8, n), I32)],
        compiler_params=_params(1, 32),
        name="router",
    )(xf, w_rt, b_r)


def _dispatch_kernel(pos_ref, x_ref, xs_ref, sem, *, tm):
    for r in range(tm):
        for k in range(TOP_K):
            pltpu.make_async_copy(x_ref.at[pl.ds(r, 1), :], xs_ref.at[pl.ds(pos_ref[k, r], 1), :], sem).start()
    for k in range(TOP_K):
        pltpu.make_async_copy(x_ref, xs_ref.at[pl.ds(0, tm), :], sem).wait()


def _dispatch(pos_tiles, xf):
    n, d = xf.shape
    tm = pos_tiles.shape[2]
    kern = functools.partial(_dispatch_kernel, tm=tm)
    return pl.pallas_call(
        kern,
        grid=(n // tm,),
        in_specs=[pl.BlockSpec((None, TOP_K, tm), lambda i: (i, 0, 0), memory_space=pltpu.SMEM),
                  pl.BlockSpec((tm, d), lambda i: (i, 0))],
        out_specs=pl.BlockSpec(memory_space=pl.ANY),
        out_shape=jax.ShapeDtypeStruct((TOP_K * n, d), F32),
        scratch_shapes=[pltpu.SemaphoreType.DMA(())],
        compiler_params=_params(1, 32),
        name="dispatch",
    )(pos_tiles, xf)


def _expert_kernel(chunk_ref, expert_ref, lo_ref, hi_ref, first_ref, nitems_ref,
                   xs_ref, wg_ref, wu_ref, wd_ref, ys_ref):
    i = pl.program_id(0)

    @pl.when(i < nitems_ref[0])
    def _():
        xb = xs_ref[...].astype(BF16)
        g = jnp.dot(xb, wg_ref[...].astype(BF16), preferred_element_type=F32)
        u = jnp.dot(xb, wu_ref[...].astype(BF16), preferred_element_type=F32)
        hid = (g * (1.0 / (1.0 + jnp.exp(-g)))) * u
        y = jnp.dot(hid.astype(BF16), wd_ref[...].astype(BF16), preferred_element_type=F32)
        row = lax.broadcasted_iota(I32, y.shape, 0)
        mine = jnp.logical_and(row >= lo_ref[i], row < hi_ref[i])

        @pl.when(first_ref[i] == 1)
        def _():
            ys_ref[...] = jnp.where(mine, y, 0.0)

        @pl.when(first_ref[i] == 0)
        def _():
            ys_ref[...] = jnp.where(mine, y, ys_ref[...])


def _experts(meta, xs, w_gate, w_up, w_down, layer):
    rows, d = xs.shape
    de = w_gate.shape[3]
    c = EXPERT_CHUNK
    n_items = meta[0].shape[0]
    grid_spec = pltpu.PrefetchScalarGridSpec(
        num_scalar_prefetch=6,
        grid=(n_items,),
        in_specs=[pl.BlockSpec((c, d), lambda i, ch, ex, lo, hi, fi, ni: (ch[i], 0)),
                  pl.BlockSpec((None, None, d, de), lambda i, ch, ex, lo, hi, fi, ni: (layer, ex[i], 0, 0)),
                  pl.BlockSpec((None, None, d, de), lambda i, ch, ex, lo, hi, fi, ni: (layer, ex[i], 0, 0)),
                  pl.BlockSpec((None, None, de, d), lambda i, ch, ex, lo, hi, fi, ni: (layer, ex[i], 0, 0))],
        out_specs=pl.BlockSpec((c, d), lambda i, ch, ex, lo, hi, fi, ni: (ch[i], 0)),
    )
    return pl.pallas_call(
        _expert_kernel,
        grid_spec=grid_spec,
        out_shape=jax.ShapeDtypeStruct((rows, d), F32),
        compiler_params=_params(1, 56),
        name="experts",
    )(*meta, xs, w_gate, w_up, w_down)


def _combine_ln_kernel(pos_ref, posn_ref, x_ref, g0_ref, g1_ref, gam_ref, bet_ref, ys_ref, o_ref, ob_ref,
                       ya0_ref, ya1_ref, yb0_ref, yb1_ref, sems, *, tm, alpha):
    i = pl.program_id(0)
    n_tiles = pl.num_programs(0)
    sets = ((ya0_ref, ya1_ref), (yb0_ref, yb1_ref))

    def issue(p_ref, which):
        for r in range(tm):
            for k in range(TOP_K):
                pltpu.make_async_copy(ys_ref.at[pl.ds(p_ref[k, r], 1), :], sets[which][k].at[pl.ds(r, 1), :],
                                      sems.at[which]).start()

    def finish(which):
        for k in range(TOP_K):
            pltpu.make_async_copy(ys_ref.at[pl.ds(0, tm), :], sets[which][k], sems.at[which]).wait()

    def compute(which):
        reps = x_ref.shape[1] // LANES
        ffn = _lane_tile(g0_ref[...], reps, axis=1) * sets[which][0][...]
        ffn = ffn + _lane_tile(g1_ref[...], reps, axis=1) * sets[which][1][...]
        out = _layer_norm_rows(alpha * x_ref[...] + ffn, gam_ref[...], bet_ref[...])
        o_ref[...] = out
        ob_ref[...] = out.astype(BF16)

    @pl.when(i == 0)
    def _():
        issue(pos_ref, 0)

    for which in range(2):
        @pl.when(i % 2 == which)
        def _(which=which):
            finish(which)
            issue(posn_ref, 1 - which)
            compute(which)

            @pl.when(i == n_tiles - 1)
            def _():
                finish(1 - which)


def _combine_ln(pos_tiles, xf, gate0, gate1, ys, ln_g, ln_b, layer, alpha):
    n, d = xf.shape
    n_tiles, _, tm = pos_tiles.shape
    kern = functools.partial(_combine_ln_kernel, tm=tm, alpha=alpha)
    row_spec = pl.BlockSpec((tm, d), lambda i: (i, 0))
    gate_spec = pl.BlockSpec((tm, LANES), lambda i: (i, 0))
    vec_spec = pl.BlockSpec((None, 1, d), lambda i: (layer, 0, 0))
    buf = pltpu.VMEM((tm, d), F32)
    return pl.pallas_call(
        kern,
        grid=(n_tiles,),
        in_specs=[pl.BlockSpec((None, TOP_K, tm), lambda i: (i, 0, 0), memory_space=pltpu.SMEM),
                  pl.BlockSpec((None, TOP_K, tm), lambda i: (jnp.minimum(i + 1, n_tiles - 1), 0, 0),
                               memory_space=pltpu.SMEM),
                  row_spec, gate_spec, gate_spec, vec_spec, vec_spec,
                  pl.BlockSpec(memory_space=pl.ANY)],
        out_specs=[row_spec, row_spec],
        out_shape=[jax.ShapeDtypeStruct((n, d), F32), jax.ShapeDtypeStruct((n, d), BF16)],
        scratch_shapes=[buf, buf, buf, buf, pltpu.SemaphoreType.DMA((2,))],
        compiler_params=_params(1, 48),
        name="combine_ln",
    )(pos_tiles, pos_tiles, xf, gate0, gate1, ln_g, ln_b, ys)


def _routing_tables(eid, n_items):
    flat = eid.reshape(-1)
    c = EXPERT_CHUNK
    onehot = (flat[:, None] == jnp.arange(N_EXPERTS, dtype=I32)[None, :]).astype(I32)
    csum = jnp.cumsum(onehot, axis=0)
    counts = csum[-1]
    ends = jnp.cumsum(counts)
    starts = ends - counts
    rank = jnp.sum(csum * onehot, axis=1) - 1
    pos = jnp.sum(onehot * starts[None, :], axis=1) + rank

    first_chunk = starts // c
    last_chunk = (ends - 1) // c
    items_per = jnp.where(counts > 0, last_chunk - first_chunk + 1, 0)
    item_ends = jnp.cumsum(items_per)
    total = item_ends[-1]
    it = jnp.minimum(jnp.arange(n_items, dtype=I32), total - 1)
    ex = jnp.sum((it[:, None] >= item_ends[None, :]).astype(I32), axis=1)
    ex_oh = (ex[:, None] == jnp.arange(N_EXPERTS, dtype=I32)[None, :]).astype(I32)
    pick = lambda v: jnp.sum(ex_oh * v[None, :], axis=1)
    chunk = pick(first_chunk) + it - (pick(item_ends) - pick(items_per))
    lo = jnp.maximum(pick(starts), chunk * c) - chunk * c
    hi = jnp.minimum(pick(ends), (chunk + 1) * c) - chunk * c
    prev_chunk = jnp.concatenate([jnp.full((1,), -1, I32), chunk[:-1]])
    first = (chunk != prev_chunk).astype(I32)
    meta = (chunk.astype(I32), ex.astype(I32), lo.astype(I32), hi.astype(I32), first,
            total.reshape(1).astype(I32))
    return pos.reshape(eid.shape).astype(I32), meta


def _rope_tables(seq):
    pos = jnp.arange(seq, dtype=F32)
    inv_freq = ROPE_THETA ** (-jnp.arange(0, ROPE_DIM, 2, dtype=F32) / ROPE_DIM)
    ang = pos[:, None] * inv_freq[None, :]
    cos, sin = jnp.cos(ang), jnp.sin(ang)
    ones = jnp.ones((seq, HEAD_DIM - ROPE_DIM), F32)
    zeros_half = jnp.zeros((seq, ROPE_HALF), F32)
    zeros_rest = jnp.zeros((seq, HEAD_DIM - ROPE_DIM), F32)
    c_tab = jnp.concatenate([cos, cos, ones], axis=1)
    sn_tab = jnp.concatenate([-sin, zeros_half, zeros_rest], axis=1)
    sp_tab = jnp.concatenate([zeros_half, sin, zeros_rest], axis=1)
    return c_tab, sn_tab, sp_tab


def kernel(x, w_in, b_f, diff_lambda, diff_subln_g, w_o, ln1_g, ln1_b, w_router_group,
           b_router_group, w_router_expert, b_router_expert, w_gate, w_up, w_down, ln2_g, ln2_b):
    batch, seq, d = x.shape
    depth = w_in.shape[0]
    n = batch * seq
    alpha = (2.0 * depth) ** 0.25
    fox_w = FOX_HEADS * HEAD_DIM
    diff_qk_w = 2 * DIFF_HEADS * HEAD_DIM
    gate_lo = 3 * fox_w
    gate_hi = gate_lo + FOX_HEADS
    q_blk = 3 * FOX_HEADS
    k_blk = q_blk + 2 * DIFF_HEADS
    v_blk = (k_blk + 2 * DIFF_HEADS) // 2
    rope_cols = (gate_lo, gate_lo + 2 * diff_qk_w)

    rope_c, rope_sn, rope_sp = _rope_tables(seq)
    xf = x.reshape(n, d)
    xb = xf.astype(BF16)
    n_items = (TOP_K * n) // EXPERT_CHUNK + N_EXPERTS
    tile = 256

    for layer in range(depth):
        lambda_init = 0.8 - 0.6 * math.exp(-0.3 * layer)
        wl = w_in[layer]
        w_main = jnp.concatenate([wl[:, :gate_lo], wl[:, gate_hi:]], axis=1).astype(BF16)
        w_f = jnp.pad(wl[:, gate_lo:gate_hi], ((0, 0), (0, LANES - FOX_HEADS))).astype(BF16)
        bias_f = jnp.pad(b_f[layer], (0, LANES - FOX_HEADS)).reshape(1, LANES)

        proj = _inproj(xb, w_main, rope_c, rope_sn, rope_sp, seq, rope_cols)
        qaug, kaug = _gate(xb, w_f, bias_f, seq)
        fox_out = _fox_attention(proj, qaug, kaug, batch, seq)
        diff_out = _diff_attention(proj, diff_lambda, diff_subln_g.reshape(depth, 1, -1), layer,
                                   lambda_init, batch, seq, q_blk, k_blk, v_blk)
        x1 = _outproj_ln(fox_out, diff_out, xf, w_o[layer].astype(BF16),
                         ln1_g.reshape(depth, 1, d), ln1_b.reshape(depth, 1, d), layer, alpha)

        w_rt = jnp.concatenate([w_router_group[layer], w_router_expert[layer]], axis=1).T
        w_rt = jnp.pad(w_rt, ((0, 4), (0, 0)))
        b_r = jnp.pad(jnp.concatenate([b_router_group[layer], b_router_expert[layer]]), (0, 4)).reshape(-1, 1)
        gate8, eid8 = _router(x1, w_rt, b_r)
        pos, meta = _routing_tables(eid8[:TOP_K], n_items)
        pos_tiles = pos.reshape(TOP_K, n // tile, tile).transpose(1, 0, 2)
        xs = _dispatch(pos_tiles, x1)
        ys = _experts(meta, xs, w_gate, w_up, w_down, layer)
        gate0 = jnp.broadcast_to(gate8[0][:, None], (n, LANES))
        gate1 = jnp.broadcast_to(gate8[1][:, None], (n, LANES))
        xf, xb = _combine_ln(pos_tiles, x1, gate0, gate1, ys, ln2_g.reshape(depth, 1, d),
                             ln2_b.reshape(depth, 1, d), layer, alpha)
    return xf.reshape(batch, seq, d)
```

```python
import functools
import math

import jax
import jax.numpy as jnp
from jax import lax
from jax.experimental import pallas as pl
from jax.experimental.pallas import tpu as pltpu

F32 = jnp.float32
BF16 = jnp.bfloat16
I32 = jnp.int32

HEAD_DIM = 128
LANES = 128
FOX_HEADS = 8
DIFF_HEADS = 4
ROPE_DIM = HEAD_DIM // 4
ROPE_HALF = ROPE_DIM // 2
ROPE_THETA = 500000.0
N_GROUPS = 4
EXPERTS_PER_GROUP = 8
N_EXPERTS = N_GROUPS * EXPERTS_PER_GROUP
TOP_K = 2
LN_EPS = 1e-5
RMS_EPS = 1e-5
ATTN_SCALE = HEAD_DIM ** -0.5
EXPERT_CHUNK = 256
MIB = 1024 * 1024

NT_DIMS = (((1,), (1,)), ((), ()))


def _lane_tile(a, reps, axis=1):
    assert axis == 1
    return a if reps == 1 else jnp.concatenate([a] * reps, axis=1)


def _params(n_axes, vmem_mib):
    return pltpu.CompilerParams(
        dimension_semantics=("arbitrary",) * n_axes,
        vmem_limit_bytes=vmem_mib * MIB)


def _inproj_kernel(x_ref, w_ref, c_ref, sn_ref, sp_ref, o_ref, *, rope_lo, rope_hi):
    j = pl.program_id(1)
    is_rope = jnp.logical_and(j >= rope_lo, j < rope_hi)

    @pl.when(is_rope)
    def _():
        acc = jnp.dot(x_ref[...], w_ref[...], preferred_element_type=F32)
        c = c_ref[...]
        sn = sn_ref[...]
        sp = sp_ref[...]
        for h in range(acc.shape[1] // HEAD_DIM):
            xh = acc[:, h * HEAD_DIM:(h + 1) * HEAD_DIM]
            up = pltpu.roll(xh, HEAD_DIM - ROPE_HALF, axis=1)
            dn = pltpu.roll(xh, ROPE_HALF, axis=1)
            o_ref[:, h * HEAD_DIM:(h + 1) * HEAD_DIM] = (xh * c + up * sn + dn * sp).astype(o_ref.dtype)

    @pl.when(jnp.logical_not(is_rope))
    def _():
        o_ref[...] = jnp.dot(x_ref[...], w_ref[...], preferred_element_type=F32).astype(o_ref.dtype)


def _inproj(xb, w_main, rope_c, rope_sn, rope_sp, seq, rope_cols):
    n, d = xb.shape
    width = w_main.shape[1]
    tm, tn = 1024, 512
    seq_tiles = seq // tm
    kern = functools.partial(_inproj_kernel, rope_lo=rope_cols[0] // tn, rope_hi=rope_cols[1] // tn)
    tab_spec = pl.BlockSpec((tm, LANES), lambda i, j: (i % seq_tiles, 0))
    return pl.pallas_call(
        kern,
        grid=(n // tm, width // tn),
        in_specs=[pl.BlockSpec((tm, d), lambda i, j: (i, 0)),
                  pl.BlockSpec((d, tn), lambda i, j: (0, j)),
                  tab_spec, tab_spec, tab_spec],
        out_specs=pl.BlockSpec((tm, tn), lambda i, j: (i, j)),
        out_shape=jax.ShapeDtypeStruct((n, width), BF16),
        compiler_params=_params(2, 48),
        name="inproj",
    )(xb, w_main, rope_c, rope_sn, rope_sp)


def _split3_f32(a):
    hi = a.astype(BF16).astype(F32)
    r1 = a - hi
    mid = r1.astype(BF16).astype(F32)
    lo = (r1 - mid).astype(BF16).astype(F32)
    return hi, mid, lo


def _gate_kernel(x_ref, w_ref, b_ref, eq_ref, ek_ref, oq_ref, ok_ref, qa_ref, ka_ref, tri_ref, carry_ref,
                 *, seq_tiles):
    i = pl.program_id(0)
    tm = x_ref.shape[0]

    @pl.when(i == 0)
    def _():
        row = lax.broadcasted_iota(I32, (tm, tm), 0)
        col = lax.broadcasted_iota(I32, (tm, tm), 1)
        tri_ref[...] = jnp.where(col <= row, 1.0, 0.0).astype(BF16)

    @pl.when(i % seq_tiles == 0)
    def _():
        carry_ref[...] = jnp.zeros(carry_ref.shape, F32)

    z = jnp.dot(x_ref[...], w_ref[...], preferred_element_type=F32) + b_ref[...]
    log_f = -(jnp.maximum(-z, 0.0) + jnp.log1p(jnp.exp(-jnp.abs(z))))
    tri = tri_ref[...]
    cum = carry_ref[0:1, :]
    for piece in _split3_f32(log_f):
        cum = cum + jnp.dot(tri, piece.astype(BF16), preferred_element_type=F32)
    carry_ref[0:1, :] = cum[tm - 1:tm, :]
    pieces = jnp.concatenate([p.astype(BF16) for p in _split3_f32(cum * (1.0 / ATTN_SCALE))], axis=1)
    qa_ref[...] = (jnp.dot(pieces, eq_ref[...], preferred_element_type=F32) + oq_ref[...]).astype(BF16)
    ka_ref[...] = (ok_ref[...] - jnp.dot(pieces, ek_ref[...], preferred_element_type=F32)).astype(BF16)


def _placement_tables():
    width = FOX_HEADS * LANES
    rows = jnp.arange(3 * LANES, dtype=I32)[:, None]
    cols = jnp.arange(width, dtype=I32)[None, :]
    piece, head = rows // LANES, rows % LANES
    e_q = ((cols == head * LANES + piece) & (head < FOX_HEADS)).astype(BF16)
    e_k = ((cols == head * LANES + 3 + piece) & (head < FOX_HEADS)).astype(BF16)
    lane = cols % LANES
    ones_q = ((lane >= 3) & (lane < 6)).astype(F32)
    ones_k = (lane < 3).astype(F32)
    return e_q, e_k, ones_q, ones_k


def _gate(xb, w_f, b_f, seq):
    n, d = xb.shape
    tm = 1024
    width = FOX_HEADS * LANES
    kern = functools.partial(_gate_kernel, seq_tiles=seq // tm)
    out_spec = pl.BlockSpec((tm, width), lambda i: (i, 0))
    const = lambda shape: pl.BlockSpec(shape, lambda i: (0, 0))
    return pl.pallas_call(
        kern,
        grid=(n // tm,),
        in_specs=[pl.BlockSpec((tm, d), lambda i: (i, 0)), const((d, LANES)), const((1, LANES)),
                  const((3 * LANES, width)), const((3 * LANES, width)), const((1, width)), const((1, width))],
        out_specs=[out_spec, out_spec],
        out_shape=[jax.ShapeDtypeStruct((n, width), BF16), jax.ShapeDtypeStruct((n, width), BF16)],
        scratch_shapes=[pltpu.VMEM((tm, tm), BF16), pltpu.VMEM((8, LANES), F32)],
        compiler_params=_params(1, 40),
        name="forget_gate",
    )(xb, w_f, b_f, *_placement_tables())


EXP2_SCALE = ATTN_SCALE * math.log2(math.e)


def _causal_mask(tq, tk):
    row = lax.broadcasted_iota(I32, (tq, tk), 0)
    col = lax.broadcasted_iota(I32, (tq, tk), 1)
    return col <= row


def _softmax_update(s, v, m_ref, l_ref, a_ref, mask):
    tq, tk = s.shape
    if mask is not None:
        s = jnp.where(mask, s, -jnp.inf)
    m_prev = m_ref[...]
    m_new = jnp.maximum(m_prev, jnp.max(s, axis=1, keepdims=True))
    alpha = jnp.exp2(EXP2_SCALE * (m_prev - m_new))
    p = jnp.exp2(EXP2_SCALE * (s - _lane_tile(m_new, tk // LANES)))
    psum = p[:, 0:LANES]
    for t in range(1, tk // LANES):
        psum = psum + p[:, t * LANES:(t + 1) * LANES]
    l_ref[...] = alpha * l_ref[...] + psum
    a_ref[...] = _lane_tile(alpha, a_ref.shape[1] // LANES) * a_ref[...] + jnp.dot(
        p.astype(BF16), v, preferred_element_type=F32)
    m_ref[...] = m_new


def _init_softmax_state(m_ref, l_ref, a_ref):
    m_ref[...] = jnp.full(m_ref.shape, -jnp.inf, F32)
    l_ref[...] = jnp.zeros(l_ref.shape, F32)
    a_ref[...] = jnp.zeros(a_ref.shape, F32)


def _normalised(a_ref, l_ref):
    l = jnp.sum(l_ref[...], axis=1, keepdims=True)
    return a_ref[...] / l


def _softmax_update_t(s, vt, m_ref, l_ref, a_ref, mask):
    if mask is not None:
        s = jnp.where(mask, s, -jnp.inf)
    m_prev = m_ref[...]
    m_new = jnp.maximum(m_prev, jnp.max(s, axis=0, keepdims=True))
    alpha = jnp.exp2(EXP2_SCALE * (m_prev - m_new))
    p = jnp.exp2(EXP2_SCALE * (s - m_new))
    l_ref[...] = alpha * l_ref[...] + jnp.sum(p, axis=0, keepdims=True)
    a_ref[...] = alpha * a_ref[...] + jnp.dot(vt, p.astype(BF16), preferred_element_type=F32)
    m_ref[...] = m_new


def _pipelined_causal_sweep(qi, scores, update):
    scores(0, 0)

    def body(t, carry):
        j = 2 * t
        scores(j + 1, 1)
        update(j, 0, False)
        scores(j + 2, 0)
        update(j + 1, 1, False)
        return carry

    lax.fori_loop(0, qi // 2, body, 0)
    j = (qi // 2) * 2

    @pl.when(qi % 2 == 1)
    def _():
        scores(j + 1, 1)
        update(j, 0, False)
        update(j + 1, 1, True)

    @pl.when(qi % 2 == 0)
    def _():
        update(j, 0, True)


def _fox_kernel(q_ref, qa_ref, k_ref, ka_ref, vt_ref, o_ref, sa_ref, sb_ref, m_ref, l_ref, acc_ref, *, blk):
    qi = pl.program_id(2)
    _init_softmax_state(m_ref, l_ref, acc_ref)
    q = jnp.concatenate([q_ref[...], qa_ref[...]], axis=1)
    key_idx = lax.broadcasted_iota(I32, (blk, blk), 0)
    query_idx = lax.broadcasted_iota(I32, (blk, blk), 1)
    mask = key_idx <= query_idx
    s_refs = (sa_ref, sb_ref)

    def scores(j, buf):
        ks = pl.multiple_of(j * blk, blk)
        kb = jnp.concatenate([k_ref[pl.ds(ks, blk), :], ka_ref[pl.ds(ks, blk), :]], axis=1)
        s_refs[buf][...] = lax.dot_general(kb, q, NT_DIMS, preferred_element_type=F32)

    def update(j, buf, diagonal):
        ks = pl.multiple_of(j * blk, blk)
        _softmax_update_t(s_refs[buf][...], vt_ref[:, pl.ds(ks, blk)], m_ref, l_ref, acc_ref,
                          mask if diagonal else None)

    _pipelined_causal_sweep(qi, scores, update)
    o_ref[...] = (acc_ref[...] / l_ref[...]).T.astype(o_ref.dtype)


def _fox_attention(proj, qaug, kaug, v_t, batch, seq):
    n = proj.shape[0]
    blk = 512
    nq = seq // blk
    kern = functools.partial(_fox_kernel, blk=blk)
    return pl.pallas_call(
        kern,
        grid=(batch, FOX_HEADS, nq),
        in_specs=[pl.BlockSpec((blk, HEAD_DIM), lambda b, h, i: (b * nq + i, h)),
                  pl.BlockSpec((blk, HEAD_DIM), lambda b, h, i: (b * nq + i, h)),
                  pl.BlockSpec((seq, HEAD_DIM), lambda b, h, i: (b, FOX_HEADS + h)),
                  pl.BlockSpec((seq, HEAD_DIM), lambda b, h, i: (b, h)),
                  pl.BlockSpec((HEAD_DIM, seq), lambda b, h, i: (b * FOX_HEADS + h, 0))],
        out_specs=pl.BlockSpec((blk, HEAD_DIM), lambda b, h, i: (b * nq + i, h)),
        out_shape=jax.ShapeDtypeStruct((n, FOX_HEADS * HEAD_DIM), BF16),
        scratch_shapes=[pltpu.VMEM((blk, blk), F32),
                        pltpu.VMEM((blk, blk), F32),
                        pltpu.VMEM((1, blk), F32),
                        pltpu.VMEM((1, blk), F32),
                        pltpu.VMEM((HEAD_DIM, blk), F32)],
        compiler_params=_params(3, 48),
        name="fox_attention",
    )(proj, qaug, proj, kaug, v_t)


def _diff_kernel(lamv_ref, g_ref, q1_ref, q2_ref, k1_ref, k2_ref, vt_ref, o_ref,
                 s1a_ref, s1b_ref, s2a_ref, s2b_ref, m1_ref, l1_ref, a1_ref, m2_ref, l2_ref, a2_ref,
                 *, blk, lambda_init):
    qi = pl.program_id(2)
    _init_softmax_state(m1_ref, l1_ref, a1_ref)
    _init_softmax_state(m2_ref, l2_ref, a2_ref)
    key_idx = lax.broadcasted_iota(I32, (blk, blk), 0)
    query_idx = lax.broadcasted_iota(I32, (blk, blk), 1)
    mask = key_idx <= query_idx
    s1_refs = (s1a_ref, s1b_ref)
    s2_refs = (s2a_ref, s2b_ref)

    def scores(j, buf):
        ks = pl.multiple_of(j * blk, blk)
        s1_refs[buf][...] = lax.dot_general(k1_ref[pl.ds(ks, blk), :], q1_ref[...], NT_DIMS,
                                            preferred_element_type=F32)
        s2_refs[buf][...] = lax.dot_general(k2_ref[pl.ds(ks, blk), :], q2_ref[...], NT_DIMS,
                                            preferred_element_type=F32)

    def update(j, buf, diagonal):
        ks = pl.multiple_of(j * blk, blk)
        vt = vt_ref[:, pl.ds(ks, blk)]
        m = mask if diagonal else None
        _softmax_update_t(s1_refs[buf][...], vt, m1_ref, l1_ref, a1_ref, m)
        _softmax_update_t(s2_refs[buf][...], vt, m2_ref, l2_ref, a2_ref, m)

    _pipelined_causal_sweep(qi, scores, update)

    lv = lamv_ref[...]
    t1 = jnp.sum(lv[0:1, :] * lv[1:2, :], axis=1, keepdims=True)
    t2 = jnp.sum(lv[2:3, :] * lv[3:4, :], axis=1, keepdims=True)
    lam = jnp.exp(t1) - jnp.exp(t2) + lambda_init
    o = a1_ref[...] / l1_ref[...] - lam * (a2_ref[...] / l2_ref[...])
    ms = jnp.mean(o * o, axis=0, keepdims=True)
    y = (o * lax.rsqrt(ms + RMS_EPS)).T * g_ref[...]
    o_ref[...] = (y * (1.0 - lambda_init)).astype(o_ref.dtype)


def _diff_attention(proj, v_t, lam_vecs, subln_g, layer, lambda_init, batch, seq, q_blk, k_blk):
    n = proj.shape[0]
    blk = 512
    nq = seq // blk
    dv = 2 * HEAD_DIM
    kern = functools.partial(_diff_kernel, blk=blk, lambda_init=lambda_init)
    sbuf = pltpu.VMEM((blk, blk), F32)
    stat = pltpu.VMEM((1, blk), F32)
    accs = pltpu.VMEM((dv, blk), F32)
    return pl.pallas_call(
        kern,
        grid=(batch, DIFF_HEADS, nq),
        in_specs=[pl.BlockSpec((None, 4, HEAD_DIM), lambda b, h, i: (layer, 0, 0)),
                  pl.BlockSpec((None, 1, dv), lambda b, h, i: (layer, 0, 0)),
                  pl.BlockSpec((blk, HEAD_DIM), lambda b, h, i: (b * nq + i, q_blk + 2 * h)),
                  pl.BlockSpec((blk, HEAD_DIM), lambda b, h, i: (b * nq + i, q_blk + 2 * h + 1)),
                  pl.BlockSpec((seq, HEAD_DIM), lambda b, h, i: (b, k_blk + 2 * h)),
                  pl.BlockSpec((seq, HEAD_DIM), lambda b, h, i: (b, k_blk + 2 * h + 1)),
                  pl.BlockSpec((dv, seq), lambda b, h, i: (b * DIFF_HEADS + h, 0))],
        out_specs=pl.BlockSpec((blk, dv), lambda b, h, i: (b * nq + i, h)),
        out_shape=jax.ShapeDtypeStruct((n, DIFF_HEADS * dv), BF16),
        scratch_shapes=[sbuf, sbuf, sbuf, sbuf, stat, stat, accs, stat, stat, accs],
        compiler_params=_params(3, 48),
        name="diff_attention",
    )(lam_vecs, subln_g, proj, proj, proj, proj, v_t)


def _layer_norm_rows(y, g, b):
    mu = jnp.mean(y, axis=1, keepdims=True)
    yc = y - mu
    var = jnp.mean(yc * yc, axis=1, keepdims=True)
    return yc * lax.rsqrt(var + LN_EPS) * g + b


def _outproj_ln_kernel(fa_ref, da_ref, x_ref, wf_ref, wd_ref, g_ref, b_ref, o_ref, *, alpha):
    mix = jnp.dot(fa_ref[...], wf_ref[...], preferred_element_type=F32)
    mix = mix + jnp.dot(da_ref[...], wd_ref[...], preferred_element_type=F32)
    o_ref[...] = _layer_norm_rows(alpha * x_ref[...] + mix, g_ref[...], b_ref[...])


def _outproj_ln(fox_out, diff_out, xf, w_o, ln_g, ln_b, layer, alpha):
    n, d = xf.shape
    half = fox_out.shape[1]
    tm = 256
    kern = functools.partial(_outproj_ln_kernel, alpha=alpha)
    row_spec = pl.BlockSpec((tm, d), lambda i: (i, 0))
    vec_spec = pl.BlockSpec((None, 1, d), lambda i: (layer, 0, 0))
    return pl.pallas_call(
        kern,
        grid=(n // tm,),
        in_specs=[pl.BlockSpec((tm, half), lambda i: (i, 0)),
                  pl.BlockSpec((tm, half), lambda i: (i, 0)),
                  row_spec,
                  pl.BlockSpec((half, d), lambda i: (0, 0)),
                  pl.BlockSpec((half, d), lambda i: (1, 0)),
                  vec_spec, vec_spec],
        out_specs=row_spec,
        out_shape=jax.ShapeDtypeStruct((n, d), F32),
        compiler_params=_params(1, 48),
        name="outproj_ln",
    )(fox_out, diff_out, xf, w_o, w_o, ln_g, ln_b)


def _split_bf16(a):
    hi = a.astype(BF16)
    lo = (a - hi.astype(F32)).astype(BF16)
    return hi, lo


def _router_kernel(x_ref, w_ref, b_ref, gate_ref, eid_ref):
    x_hi, x_lo = _split_bf16(x_ref[...])
    w_hi, w_lo = _split_bf16(w_ref[...])
    dg = functools.partial(lax.dot_general, dimension_numbers=NT_DIMS, preferred_element_type=F32)
    lg = dg(w_hi, x_hi) + dg(w_hi, x_lo) + dg(w_lo, x_hi) + b_ref[...]

    def first_index(vals, top):
        idx = jnp.full(top.shape, len(vals) - 1, I32)
        for j in range(len(vals) - 2, -1, -1):
            idx = jnp.where(vals[j] == top, j, idx)
        return idx

    g = [lg[i:i + 1, :] for i in range(N_GROUPS)]
    gmax = functools.reduce(jnp.maximum, g)
    gsum = functools.reduce(lambda a, b: a + b, [jnp.exp(gi - gmax) for gi in g])
    g_w = 1.0 / gsum
    sel = first_index(g, gmax)

    es = []
    for j in range(EXPERTS_PER_GROUP):
        rows = [lg[N_GROUPS + gi * EXPERTS_PER_GROUP + j:N_GROUPS + gi * EXPERTS_PER_GROUP + j + 1, :]
                for gi in range(N_GROUPS)]
        e = rows[N_GROUPS - 1]
        for gi in range(N_GROUPS - 2, -1, -1):
            e = jnp.where(sel == gi, rows[gi], e)
        es.append(e)
    emax = functools.reduce(jnp.maximum, es)
    ex = [jnp.exp(e - emax) for e in es]
    esum = functools.reduce(lambda a, b: a + b, ex)
    prob = [e / esum for e in ex]
    v1 = functools.reduce(jnp.maximum, prob)
    i1 = first_index(prob, v1)
    rest = [jnp.where(i1 == j, -1.0, prob[j]) for j in range(EXPERTS_PER_GROUP)]
    v2 = functools.reduce(jnp.maximum, rest)
    i2 = first_index(rest, v2)
    den = v1 + v2
    gate_ref[...] = jnp.zeros(gate_ref.shape, F32)
    eid_ref[...] = jnp.zeros(eid_ref.shape, I32)
    gate_ref[0:1, :] = g_w * (v1 / den)
    gate_ref[1:2, :] = g_w * (v2 / den)
    eid_ref[0:1, :] = sel * EXPERTS_PER_GROUP + i1
    eid_ref[1:2, :] = sel * EXPERTS_PER_GROUP + i2


def _router(xf, w_rt, b_r):
    n, d = xf.shape
    rows = w_rt.shape[0]
    tm = 512
    return pl.pallas_call(
        _router_kernel,
        grid=(n // tm,),
        in_specs=[pl.BlockSpec((tm, d), lambda i: (i, 0)),
                  pl.BlockSpec((rows, d), lambda i: (0, 0)),
                  pl.BlockSpec((rows, 1), lambda i: (0, 0))],
        out_specs=[pl.BlockSpec((8, tm), lambda i: (0, i)),
                   pl.BlockSpec((8, tm), lambda i: (0, i))],
        out_shape=[jax.ShapeDtypeStruct((8, n), F32), jax.ShapeDtypeStruct((8, n), I32)],
        compiler_params=_params(1, 32),
        name="router",
    )(xf, w_rt, b_r)


def _dispatch_kernel(pos_ref, x_ref, xs_ref, sem, *, tm):
    for r in range(tm):
        for k in range(TOP_K):
            pltpu.make_async_copy(x_ref.at[pl.ds(r, 1), :], xs_ref.at[pl.ds(pos_ref[k, r], 1), :], sem).start()
    for k in range(TOP_K):
        pltpu.make_async_copy(x_ref, xs_ref.at[pl.ds(0, tm), :], sem).wait()


def _dispatch(pos_tiles, xf):
    n, d = xf.shape
    tm = pos_tiles.shape[2]
    kern = functools.partial(_dispatch_kernel, tm=tm)
    return pl.pallas_call(
        kern,
        grid=(n // tm,),
        in_specs=[pl.BlockSpec((None, TOP_K, tm), lambda i: (i, 0, 0), memory_space=pltpu.SMEM),
                  pl.BlockSpec((tm, d), lambda i: (i, 0))],
        out_specs=pl.BlockSpec(memory_space=pl.ANY),
        out_shape=jax.ShapeDtypeStruct((TOP_K * n, d), F32),
        scratch_shapes=[pltpu.SemaphoreType.DMA(())],
        compiler_params=_params(1, 32),
        name="dispatch",
    )(pos_tiles, xf)


def _expert_kernel(chunk_ref, expert_ref, lo_ref, hi_ref, first_ref, nitems_ref,
                   xs_ref, wg_ref, wu_ref, wd_ref, ys_ref):
    i = pl.program_id(0)

    @pl.when(i < nitems_ref[0])
    def _():
        xb = xs_ref[...].astype(BF16)
        g = jnp.dot(xb, wg_ref[...].astype(BF16), preferred_element_type=F32)
        u = jnp.dot(xb, wu_ref[...].astype(BF16), preferred_element_type=F32)
        hid = (g * (1.0 / (1.0 + jnp.exp(-g)))) * u
        y = jnp.dot(hid.astype(BF16), wd_ref[...].astype(BF16), preferred_element_type=F32)
        row = lax.broadcasted_iota(I32, y.shape, 0)
        mine = jnp.logical_and(row >= lo_ref[i], row < hi_ref[i])

        @pl.when(first_ref[i] == 1)
        def _():
            ys_ref[...] = jnp.where(mine, y, 0.0)

        @pl.when(first_ref[i] == 0)
        def _():
            ys_ref[...] = jnp.where(mine, y, ys_ref[...])


def _experts(meta, xs, w_gate, w_up, w_down, layer):
    rows, d = xs.shape
    de = w_gate.shape[3]
    c = EXPERT_CHUNK
    n_items = meta[0].shape[0]
    grid_spec = pltpu.PrefetchScalarGridSpec(
        num_scalar_prefetch=6,
        grid=(n_items,),
        in_specs=[pl.BlockSpec((c, d), lambda i, ch, ex, lo, hi, fi, ni: (ch[i], 0)),
                  pl.BlockSpec((None, None, d, de), lambda i, ch, ex, lo, hi, fi, ni: (layer, ex[i], 0, 0)),
                  pl.BlockSpec((None, None, d, de), lambda i, ch, ex, lo, hi, fi, ni: (layer, ex[i], 0, 0)),
                  pl.BlockSpec((None, None, de, d), lambda i, ch, ex, lo, hi, fi, ni: (layer, ex[i], 0, 0))],
        out_specs=pl.BlockSpec((c, d), lambda i, ch, ex, lo, hi, fi, ni: (ch[i], 0)),
    )
    return pl.pallas_call(
        _expert_kernel,
        grid_spec=grid_spec,
        out_shape=jax.ShapeDtypeStruct((rows, d), F32),
        compiler_params=_params(1, 56),
        name="experts",
    )(*meta, xs, w_gate, w_up, w_down)


def _combine_ln_kernel(pos_ref, posn_ref, x_ref, g0_ref, g1_ref, gam_ref, bet_ref, ys_ref, o_ref, ob_ref,
                       ya0_ref, ya1_ref, yb0_ref, yb1_ref, sems, *, tm, alpha):
    i = pl.program_id(0)
    n_tiles = pl.num_programs(0)
    sets = ((ya0_ref, ya1_ref), (yb0_ref, yb1_ref))

    def issue(p_ref, which):
        for r in range(tm):
            for k in range(TOP_K):
                pltpu.make_async_copy(ys_ref.at[pl.ds(p_ref[k, r], 1), :], sets[which][k].at[pl.ds(r, 1), :],
                                      sems.at[which]).start()

    def finish(which):
        for k in range(TOP_K):
            pltpu.make_async_copy(ys_ref.at[pl.ds(0, tm), :], sets[which][k], sems.at[which]).wait()

    def compute(which):
        reps = x_ref.shape[1] // LANES
        ffn = _lane_tile(g0_ref[...], reps, axis=1) * sets[which][0][...]
        ffn = ffn + _lane_tile(g1_ref[...], reps, axis=1) * sets[which][1][...]
        out = _layer_norm_rows(alpha * x_ref[...] + ffn, gam_ref[...], bet_ref[...])
        o_ref[...] = out
        ob_ref[...] = out.astype(BF16)

    @pl.when(i == 0)
    def _():
        issue(pos_ref, 0)

    for which in range(2):
        @pl.when(i % 2 == which)
        def _(which=which):
            finish(which)
            issue(posn_ref, 1 - which)
            compute(which)

            @pl.when(i == n_tiles - 1)
            def _():
                finish(1 - which)


def _combine_ln(pos_tiles, xf, gate0, gate1, ys, ln_g, ln_b, layer, alpha):
    n, d = xf.shape
    n_tiles, _, tm = pos_tiles.shape
    kern = functools.partial(_combine_ln_kernel, tm=tm, alpha=alpha)
    row_spec = pl.BlockSpec((tm, d), lambda i: (i, 0))
    gate_spec = pl.BlockSpec((tm, LANES), lambda i: (i, 0))
    vec_spec = pl.BlockSpec((None, 1, d), lambda i: (layer, 0, 0))
    buf = pltpu.VMEM((tm, d), F32)
    return pl.pallas_call(
        kern,
        grid=(n_tiles,),
        in_specs=[pl.BlockSpec((None, TOP_K, tm), lambda i: (i, 0, 0), memory_space=pltpu.SMEM),
                  pl.BlockSpec((None, TOP_K, tm), lambda i: (jnp.minimum(i + 1, n_tiles - 1), 0, 0),
                               memory_space=pltpu.SMEM),
                  row_spec, gate_spec, gate_spec, vec_spec, vec_spec,
                  pl.BlockSpec(memory_space=pl.ANY)],
        out_specs=[row_spec, row_spec],
        out_shape=[jax.ShapeDtypeStruct((n, d), F32), jax.ShapeDtypeStruct((n, d), BF16)],
        scratch_shapes=[buf, buf, buf, buf, pltpu.SemaphoreType.DMA((2,))],
        compiler_params=_params(1, 48),
        name="combine_ln",
    )(pos_tiles, pos_tiles, xf, gate0, gate1, ln_g, ln_b, ys)


def _routing_tables(eid, n_items):
    flat = eid.reshape(-1)
    c = EXPERT_CHUNK
    onehot = (flat[:, None] == jnp.arange(N_EXPERTS, dtype=I32)[None, :]).astype(I32)
    csum = jnp.cumsum(onehot, axis=0)
    counts = csum[-1]
    ends = jnp.cumsum(counts)
    starts = ends - counts
    rank = jnp.sum(csum * onehot, axis=1) - 1
    pos = jnp.sum(onehot * starts[None, :], axis=1) + rank

    first_chunk = starts // c
    last_chunk = (ends - 1) // c
    items_per = jnp.where(counts > 0, last_chunk - first_chunk + 1, 0)
    item_ends = jnp.cumsum(items_per)
    total = item_ends[-1]
    it = jnp.minimum(jnp.arange(n_items, dtype=I32), total - 1)
    ex = jnp.sum((it[:, None] >= item_ends[None, :]).astype(I32), axis=1)
    ex_oh = (ex[:, None] == jnp.arange(N_EXPERTS, dtype=I32)[None, :]).astype(I32)
    pick = lambda v: jnp.sum(ex_oh * v[None, :], axis=1)
    chunk = pick(first_chunk) + it - (pick(item_ends) - pick(items_per))
    lo = jnp.maximum(pick(starts), chunk * c) - chunk * c
    hi = jnp.minimum(pick(ends), (chunk + 1) * c) - chunk * c
    prev_chunk = jnp.concatenate([jnp.full((1,), -1, I32), chunk[:-1]])
    first = (chunk != prev_chunk).astype(I32)
    meta = (chunk.astype(I32), ex.astype(I32), lo.astype(I32), hi.astype(I32), first,
            total.reshape(1).astype(I32))
    return pos.reshape(eid.shape).astype(I32), meta


def _rope_tables(seq):
    pos = jnp.arange(seq, dtype=F32)
    inv_freq = ROPE_THETA ** (-jnp.arange(0, ROPE_DIM, 2, dtype=F32) / ROPE_DIM)
    ang = pos[:, None] * inv_freq[None, :]
    cos, sin = jnp.cos(ang), jnp.sin(ang)
    ones = jnp.ones((seq, HEAD_DIM - ROPE_DIM), F32)
    zeros_half = jnp.zeros((seq, ROPE_HALF), F32)
    zeros_rest = jnp.zeros((seq, HEAD_DIM - ROPE_DIM), F32)
    c_tab = jnp.concatenate([cos, cos, ones], axis=1)
    sn_tab = jnp.concatenate([-sin, zeros_half, zeros_rest], axis=1)
    sp_tab = jnp.concatenate([zeros_half, sin, zeros_rest], axis=1)
    return c_tab, sn_tab, sp_tab


def kernel(x, w_in, b_f, diff_lambda, diff_subln_g, w_o, ln1_g, ln1_b, w_router_group,
           b_router_group, w_router_expert, b_router_expert, w_gate, w_up, w_down, ln2_g, ln2_b):
    batch, seq, d = x.shape
    depth = w_in.shape[0]
    n = batch * seq
    alpha = (2.0 * depth) ** 0.25
    fox_w = FOX_HEADS * HEAD_DIM
    diff_qk_w = 2 * DIFF_HEADS * HEAD_DIM
    gate_lo = 3 * fox_w
    gate_hi = gate_lo + FOX_HEADS
    q_blk = 3 * FOX_HEADS
    k_blk = q_blk + 2 * DIFF_HEADS
    v_cols = (k_blk + 2 * DIFF_HEADS) * HEAD_DIM
    rope_cols = (gate_lo, gate_lo + 2 * diff_qk_w)

    rope_c, rope_sn, rope_sp = _rope_tables(seq)
    xf = x.reshape(n, d)
    xb = xf.astype(BF16)
    n_items = (TOP_K * n) // EXPERT_CHUNK + N_EXPERTS
    tile = 256

    for layer in range(depth):
        lambda_init = 0.8 - 0.6 * math.exp(-0.3 * layer)
        wl = w_in[layer]
        w_main = jnp.concatenate([wl[:, :gate_lo], wl[:, gate_hi:]], axis=1).astype(BF16)
        w_f = jnp.pad(wl[:, gate_lo:gate_hi], ((0, 0), (0, LANES - FOX_HEADS))).astype(BF16)
        bias_f = jnp.pad(b_f[layer], (0, LANES - FOX_HEADS)).reshape(1, LANES)

        proj = _inproj(xb, w_main, rope_c, rope_sn, rope_sp, seq, rope_cols)
        qaug, kaug = _gate(xb, w_f, bias_f, seq)
        fv = proj[:, 2 * fox_w:3 * fox_w].reshape(batch, seq, FOX_HEADS, HEAD_DIM)
        fv_t = fv.transpose(0, 2, 3, 1).reshape(batch * FOX_HEADS * HEAD_DIM, seq)
        fox_out = _fox_attention(proj, qaug, kaug, fv_t, batch, seq)
        dv_t = proj[:, v_cols:].reshape(batch, seq, DIFF_HEADS * 2 * HEAD_DIM).transpose(0, 2, 1)
        dv_t = dv_t.reshape(batch * DIFF_HEADS * 2 * HEAD_DIM, seq)
        diff_out = _diff_attention(proj, dv_t, diff_lambda, diff_subln_g.reshape(depth, 1, -1), layer,
                                   lambda_init, batch, seq, q_blk, k_blk)
        x1 = _outproj_ln(fox_out, diff_out, xf, w_o[layer].astype(BF16),
                         ln1_g.reshape(depth, 1, d), ln1_b.reshape(depth, 1, d), layer, alpha)

        w_rt = jnp.concatenate([w_router_group[layer], w_router_expert[layer]], axis=1).T
        w_rt = jnp.pad(w_rt, ((0, 4), (0, 0)))
        b_r = jnp.pad(jnp.concatenate([b_router_group[layer], b_router_expert[layer]]), (0, 4)).reshape(-1, 1)
        gate8, eid8 = _router(x1, w_rt, b_r)
        pos, meta = _routing_tables(eid8[:TOP_K], n_items)
        pos_tiles = pos.reshape(TOP_K, n // tile, tile).transpose(1, 0, 2)
        xs = _dispatch(pos_tiles, x1)
        ys = _experts(meta, xs, w_gate, w_up, w_down, layer)
        gate0 = jnp.broadcast_to(gate8[0][:, None], (n, LANES))
        gate1 = jnp.broadcast_to(gate8[1][:, None], (n, LANES))
        xf, xb = _combine_ln(pos_tiles, x1, gate0, gate1, ys, ln2_g.reshape(depth, 1, d),
                             ln2_b.reshape(depth, 1, d), layer, alpha)
    return xf.reshape(batch, seq, d)
```

```python
import functools
import math

import jax
import jax.numpy as jnp
from jax import lax
from jax.experimental import pallas as pl
from jax.experimental.pallas import tpu as pltpu

F32 = jnp.float32
BF16 = jnp.bfloat16
I32 = jnp.int32

HEAD_DIM = 128
LANES = 128
FOX_HEADS = 8
DIFF_HEADS = 4
ROPE_DIM = HEAD_DIM // 4
ROPE_HALF = ROPE_DIM // 2
ROPE_THETA = 500000.0
N_GROUPS = 4
EXPERTS_PER_GROUP = 8
N_EXPERTS = N_GROUPS * EXPERTS_PER_GROUP
TOP_K = 2
LN_EPS = 1e-5
RMS_EPS = 1e-5
ATTN_SCALE = HEAD_DIM ** -0.5
EXPERT_CHUNK = 256
MIB = 1024 * 1024

NT_DIMS = (((1,), (1,)), ((), ()))


def _lane_tile(a, reps, axis=1):
    assert axis == 1
    return a if reps == 1 else jnp.concatenate([a] * reps, axis=1)


def _params(n_axes, vmem_mib):
    return pltpu.CompilerParams(
        dimension_semantics=("arbitrary",) * n_axes,
        vmem_limit_bytes=vmem_mib * MIB)


def _inproj_kernel(x_ref, w_ref, c_ref, sn_ref, sp_ref, o_ref, *, rope_lo, rope_hi):
    j = pl.program_id(1)
    is_rope = jnp.logical_and(j >= rope_lo, j < rope_hi)

    @pl.when(is_rope)
    def _():
        acc = jnp.dot(x_ref[...], w_ref[...], preferred_element_type=F32)
        c = c_ref[...]
        sn = sn_ref[...]
        sp = sp_ref[...]
        for h in range(acc.shape[1] // HEAD_DIM):
            xh = acc[:, h * HEAD_DIM:(h + 1) * HEAD_DIM]
            up = pltpu.roll(xh, HEAD_DIM - ROPE_HALF, axis=1)
            dn = pltpu.roll(xh, ROPE_HALF, axis=1)
            o_ref[:, h * HEAD_DIM:(h + 1) * HEAD_DIM] = (xh * c + up * sn + dn * sp).astype(o_ref.dtype)

    @pl.when(jnp.logical_not(is_rope))
    def _():
        o_ref[...] = jnp.dot(x_ref[...], w_ref[...], preferred_element_type=F32).astype(o_ref.dtype)


def _inproj(xb, w_main, rope_c, rope_sn, rope_sp, seq, rope_cols):
    n, d = xb.shape
    width = w_main.shape[1]
    tm, tn = 1024, 512
    seq_tiles = seq // tm
    kern = functools.partial(_inproj_kernel, rope_lo=rope_cols[0] // tn, rope_hi=rope_cols[1] // tn)
    tab_spec = pl.BlockSpec((tm, LANES), lambda i, j: (i % seq_tiles, 0))
    return pl.pallas_call(
        kern,
        grid=(n // tm, width // tn),
        in_specs=[pl.BlockSpec((tm, d), lambda i, j: (i, 0)),
                  pl.BlockSpec((d, tn), lambda i, j: (0, j)),
                  tab_spec, tab_spec, tab_spec],
        out_specs=pl.BlockSpec((tm, tn), lambda i, j: (i, j)),
        out_shape=jax.ShapeDtypeStruct((n, width), BF16),
        compiler_params=_params(2, 48),
        name="inproj",
    )(xb, w_main, rope_c, rope_sn, rope_sp)


def _inproj_t_kernel(x_ref, wt_ref, o_ref):
    o_ref[...] = lax.dot_general(wt_ref[...], x_ref[...], NT_DIMS,
                                 preferred_element_type=F32).astype(o_ref.dtype)


def _inproj_t(xb, w_t):
    n, d = xb.shape
    cols = w_t.shape[0]
    tm, tn = 1024, 512
    return pl.pallas_call(
        _inproj_t_kernel,
        grid=(n // tm, cols // tn),
        in_specs=[pl.BlockSpec((tm, d), lambda i, j: (i, 0)),
                  pl.BlockSpec((tn, d), lambda i, j: (j, 0))],
        out_specs=pl.BlockSpec((tn, tm), lambda i, j: (j, i)),
        out_shape=jax.ShapeDtypeStruct((cols, n), BF16),
        compiler_params=_params(2, 48),
        name="inproj_values_t",
    )(xb, w_t)


def _split3_f32(a):
    hi = a.astype(BF16).astype(F32)
    r1 = a - hi
    mid = r1.astype(BF16).astype(F32)
    lo = (r1 - mid).astype(BF16).astype(F32)
    return hi, mid, lo


def _gate_kernel(x_ref, w_ref, b_ref, eq_ref, ek_ref, oq_ref, ok_ref, qa_ref, ka_ref, tri_ref, carry_ref,
                 *, seq_tiles):
    i = pl.program_id(0)
    tm = x_ref.shape[0]

    @pl.when(i == 0)
    def _():
        row = lax.broadcasted_iota(I32, (tm, tm), 0)
        col = lax.broadcasted_iota(I32, (tm, tm), 1)
        tri_ref[...] = jnp.where(col <= row, 1.0, 0.0).astype(BF16)

    @pl.when(i % seq_tiles == 0)
    def _():
        carry_ref[...] = jnp.zeros(carry_ref.shape, F32)

    z = jnp.dot(x_ref[...], w_ref[...], preferred_element_type=F32) + b_ref[...]
    log_f = -(jnp.maximum(-z, 0.0) + jnp.log1p(jnp.exp(-jnp.abs(z))))
    tri = tri_ref[...]
    cum = carry_ref[0:1, :]
    for piece in _split3_f32(log_f):
        cum = cum + jnp.dot(tri, piece.astype(BF16), preferred_element_type=F32)
    carry_ref[0:1, :] = cum[tm - 1:tm, :]
    pieces = jnp.concatenate([p.astype(BF16) for p in _split3_f32(cum * (1.0 / ATTN_SCALE))], axis=1)
    qa_ref[...] = (jnp.dot(pieces, eq_ref[...], preferred_element_type=F32) + oq_ref[...]).astype(BF16)
    ka_ref[...] = (ok_ref[...] - jnp.dot(pieces, ek_ref[...], preferred_element_type=F32)).astype(BF16)


def _placement_tables():
    width = FOX_HEADS * LANES
    rows = jnp.arange(3 * LANES, dtype=I32)[:, None]
    cols = jnp.arange(width, dtype=I32)[None, :]
    piece, head = rows // LANES, rows % LANES
    e_q = ((cols == head * LANES + piece) & (head < FOX_HEADS)).astype(BF16)
    e_k = ((cols == head * LANES + 3 + piece) & (head < FOX_HEADS)).astype(BF16)
    lane = cols % LANES
    ones_q = ((lane >= 3) & (lane < 6)).astype(F32)
    ones_k = (lane < 3).astype(F32)
    return e_q, e_k, ones_q, ones_k


def _gate(xb, w_f, b_f, seq):
    n, d = xb.shape
    tm = 1024
    width = FOX_HEADS * LANES
    kern = functools.partial(_gate_kernel, seq_tiles=seq // tm)
    out_spec = pl.BlockSpec((tm, width), lambda i: (i, 0))
    const = lambda shape: pl.BlockSpec(shape, lambda i: (0, 0))
    return pl.pallas_call(
        kern,
        grid=(n // tm,),
        in_specs=[pl.BlockSpec((tm, d), lambda i: (i, 0)), const((d, LANES)), const((1, LANES)),
                  const((3 * LANES, width)), const((3 * LANES, width)), const((1, width)), const((1, width))],
        out_specs=[out_spec, out_spec],
        out_shape=[jax.ShapeDtypeStruct((n, width), BF16), jax.ShapeDtypeStruct((n, width), BF16)],
        scratch_shapes=[pltpu.VMEM((tm, tm), BF16), pltpu.VMEM((8, LANES), F32)],
        compiler_params=_params(1, 40),
        name="forget_gate",
    )(xb, w_f, b_f, *_placement_tables())


EXP2_SCALE = ATTN_SCALE * math.log2(math.e)


def _causal_mask(tq, tk):
    row = lax.broadcasted_iota(I32, (tq, tk), 0)
    col = lax.broadcasted_iota(I32, (tq, tk), 1)
    return col <= row


def _softmax_update(s, v, m_ref, l_ref, a_ref, mask):
    tq, tk = s.shape
    if mask is not None:
        s = jnp.where(mask, s, -jnp.inf)
    m_prev = m_ref[...]
    m_new = jnp.maximum(m_prev, jnp.max(s, axis=1, keepdims=True))
    alpha = jnp.exp2(EXP2_SCALE * (m_prev - m_new))
    p = jnp.exp2(EXP2_SCALE * (s - _lane_tile(m_new, tk // LANES)))
    psum = p[:, 0:LANES]
    for t in range(1, tk // LANES):
        psum = psum + p[:, t * LANES:(t + 1) * LANES]
    l_ref[...] = alpha * l_ref[...] + psum
    a_ref[...] = _lane_tile(alpha, a_ref.shape[1] // LANES) * a_ref[...] + jnp.dot(
        p.astype(BF16), v, preferred_element_type=F32)
    m_ref[...] = m_new


def _init_softmax_state(m_ref, l_ref, a_ref):
    m_ref[...] = jnp.full(m_ref.shape, -jnp.inf, F32)
    l_ref[...] = jnp.zeros(l_ref.shape, F32)
    a_ref[...] = jnp.zeros(a_ref.shape, F32)


def _normalised(a_ref, l_ref):
    l = jnp.sum(l_ref[...], axis=1, keepdims=True)
    return a_ref[...] / l


def _softmax_update_t(s, vt, m_ref, l_ref, a_ref, mask):
    if mask is not None:
        s = jnp.where(mask, s, -jnp.inf)
    m_prev = m_ref[...]
    m_new = jnp.maximum(m_prev, jnp.max(s, axis=0, keepdims=True))
    alpha = jnp.exp2(EXP2_SCALE * (m_prev - m_new))
    p = jnp.exp2(EXP2_SCALE * (s - m_new))
    l_ref[...] = alpha * l_ref[...] + jnp.sum(p, axis=0, keepdims=True)
    a_ref[...] = alpha * a_ref[...] + jnp.dot(vt, p.astype(BF16), preferred_element_type=F32)
    m_ref[...] = m_new


def _pipelined_causal_sweep(qi, scores, update):
    scores(0, 0)

    def body(t, carry):
        j = 2 * t
        scores(j + 1, 1)
        update(j, 0, False)
        scores(j + 2, 0)
        update(j + 1, 1, False)
        return carry

    lax.fori_loop(0, qi // 2, body, 0)
    j = (qi // 2) * 2

    @pl.when(qi % 2 == 1)
    def _():
        scores(j + 1, 1)
        update(j, 0, False)
        update(j + 1, 1, True)

    @pl.when(qi % 2 == 0)
    def _():
        update(j, 0, True)


def _fox_kernel(q_ref, qa_ref, k_ref, ka_ref, vt_ref, o_ref, sa_ref, sb_ref, m_ref, l_ref, acc_ref, *, blk):
    qi = pl.program_id(2)
    _init_softmax_state(m_ref, l_ref, acc_ref)
    q = jnp.concatenate([q_ref[...], qa_ref[...]], axis=1)
    key_idx = lax.broadcasted_iota(I32, (blk, blk), 0)
    query_idx = lax.broadcasted_iota(I32, (blk, blk), 1)
    mask = key_idx <= query_idx
    s_refs = (sa_ref, sb_ref)

    def scores(j, buf):
        ks = pl.multiple_of(j * blk, blk)
        kb = jnp.concatenate([k_ref[pl.ds(ks, blk), :], ka_ref[pl.ds(ks, blk), :]], axis=1)
        s_refs[buf][...] = lax.dot_general(kb, q, NT_DIMS, preferred_element_type=F32)

    def update(j, buf, diagonal):
        ks = pl.multiple_of(j * blk, blk)
        _softmax_update_t(s_refs[buf][...], vt_ref[:, pl.ds(ks, blk)], m_ref, l_ref, acc_ref,
                          mask if diagonal else None)

    _pipelined_causal_sweep(qi, scores, update)
    o_ref[...] = (acc_ref[...] / l_ref[...]).T.astype(o_ref.dtype)


def _fox_attention(proj, qaug, kaug, v_t, batch, seq):
    n = proj.shape[0]
    blk = 512
    nq = seq // blk
    kern = functools.partial(_fox_kernel, blk=blk)
    return pl.pallas_call(
        kern,
        grid=(batch, FOX_HEADS, nq),
        in_specs=[pl.BlockSpec((blk, HEAD_DIM), lambda b, h, i: (b * nq + i, h)),
                  pl.BlockSpec((blk, HEAD_DIM), lambda b, h, i: (b * nq + i, h)),
                  pl.BlockSpec((seq, HEAD_DIM), lambda b, h, i: (b, FOX_HEADS + h)),
                  pl.BlockSpec((seq, HEAD_DIM), lambda b, h, i: (b, h)),
                  pl.BlockSpec((HEAD_DIM, seq), lambda b, h, i: (h, b))],
        out_specs=pl.BlockSpec((blk, HEAD_DIM), lambda b, h, i: (b * nq + i, h)),
        out_shape=jax.ShapeDtypeStruct((n, FOX_HEADS * HEAD_DIM), BF16),
        scratch_shapes=[pltpu.VMEM((blk, blk), F32),
                        pltpu.VMEM((blk, blk), F32),
                        pltpu.VMEM((1, blk), F32),
                        pltpu.VMEM((1, blk), F32),
                        pltpu.VMEM((HEAD_DIM, blk), F32)],
        compiler_params=_params(3, 48),
        name="fox_attention",
    )(proj, qaug, proj, kaug, v_t)


def _diff_kernel(lamv_ref, g_ref, q1_ref, q2_ref, k1_ref, k2_ref, vt_ref, o_ref,
                 s1a_ref, s1b_ref, s2a_ref, s2b_ref, m1_ref, l1_ref, a1_ref, m2_ref, l2_ref, a2_ref,
                 *, blk, lambda_init):
    qi = pl.program_id(2)
    _init_softmax_state(m1_ref, l1_ref, a1_ref)
    _init_softmax_state(m2_ref, l2_ref, a2_ref)
    key_idx = lax.broadcasted_iota(I32, (blk, blk), 0)
    query_idx = lax.broadcasted_iota(I32, (blk, blk), 1)
    mask = key_idx <= query_idx
    s1_refs = (s1a_ref, s1b_ref)
    s2_refs = (s2a_ref, s2b_ref)

    def scores(j, buf):
        ks = pl.multiple_of(j * blk, blk)
        s1_refs[buf][...] = lax.dot_general(k1_ref[pl.ds(ks, blk), :], q1_ref[...], NT_DIMS,
                                            preferred_element_type=F32)
        s2_refs[buf][...] = lax.dot_general(k2_ref[pl.ds(ks, blk), :], q2_ref[...], NT_DIMS,
                                            preferred_element_type=F32)

    def update(j, buf, diagonal):
        ks = pl.multiple_of(j * blk, blk)
        vt = vt_ref[:, pl.ds(ks, blk)]
        m = mask if diagonal else None
        _softmax_update_t(s1_refs[buf][...], vt, m1_ref, l1_ref, a1_ref, m)
        _softmax_update_t(s2_refs[buf][...], vt, m2_ref, l2_ref, a2_ref, m)

    _pipelined_causal_sweep(qi, scores, update)

    lv = lamv_ref[...]
    t1 = jnp.sum(lv[0:1, :] * lv[1:2, :], axis=1, keepdims=True)
    t2 = jnp.sum(lv[2:3, :] * lv[3:4, :], axis=1, keepdims=True)
    lam = jnp.exp(t1) - jnp.exp(t2) + lambda_init
    o = a1_ref[...] / l1_ref[...] - lam * (a2_ref[...] / l2_ref[...])
    ms = jnp.mean(o * o, axis=0, keepdims=True)
    y = (o * lax.rsqrt(ms + RMS_EPS)).T * g_ref[...]
    o_ref[...] = (y * (1.0 - lambda_init)).astype(o_ref.dtype)


def _diff_attention(proj, v_t, lam_vecs, subln_g, layer, lambda_init, batch, seq, q_blk, k_blk):
    n = proj.shape[0]
    blk = 512
    nq = seq // blk
    dv = 2 * HEAD_DIM
    kern = functools.partial(_diff_kernel, blk=blk, lambda_init=lambda_init)
    sbuf = pltpu.VMEM((blk, blk), F32)
    stat = pltpu.VMEM((1, blk), F32)
    accs = pltpu.VMEM((dv, blk), F32)
    return pl.pallas_call(
        kern,
        grid=(batch, DIFF_HEADS, nq),
        in_specs=[pl.BlockSpec((None, 4, HEAD_DIM), lambda b, h, i: (layer, 0, 0)),
                  pl.BlockSpec((None, 1, dv), lambda b, h, i: (layer, 0, 0)),
                  pl.BlockSpec((blk, HEAD_DIM), lambda b, h, i: (b * nq + i, q_blk + 2 * h)),
                  pl.BlockSpec((blk, HEAD_DIM), lambda b, h, i: (b * nq + i, q_blk + 2 * h + 1)),
                  pl.BlockSpec((seq, HEAD_DIM), lambda b, h, i: (b, k_blk + 2 * h)),
                  pl.BlockSpec((seq, HEAD_DIM), lambda b, h, i: (b, k_blk + 2 * h + 1)),
                  pl.BlockSpec((dv, seq), lambda b, h, i: (FOX_HEADS * HEAD_DIM // dv + h, b))],
        out_specs=pl.BlockSpec((blk, dv), lambda b, h, i: (b * nq + i, h)),
        out_shape=jax.ShapeDtypeStruct((n, DIFF_HEADS * dv), BF16),
        scratch_shapes=[sbuf, sbuf, sbuf, sbuf, stat, stat, accs, stat, stat, accs],
        compiler_params=_params(3, 48),
        name="diff_attention",
    )(lam_vecs, subln_g, proj, proj, proj, proj, v_t)


def _layer_norm_rows(y, g, b):
    mu = jnp.mean(y, axis=1, keepdims=True)
    yc = y - mu
    var = jnp.mean(yc * yc, axis=1, keepdims=True)
    return yc * lax.rsqrt(var + LN_EPS) * g + b


def _outproj_ln_kernel(fa_ref, da_ref, x_ref, wf_ref, wd_ref, g_ref, b_ref, o_ref, *, alpha):
    mix = jnp.dot(fa_ref[...], wf_ref[...], preferred_element_type=F32)
    mix = mix + jnp.dot(da_ref[...], wd_ref[...], preferred_element_type=F32)
    o_ref[...] = _layer_norm_rows(alpha * x_ref[...] + mix, g_ref[...], b_ref[...])


def _outproj_ln(fox_out, diff_out, xf, w_o, ln_g, ln_b, layer, alpha):
    n, d = xf.shape
    half = fox_out.shape[1]
    tm = 256
    kern = functools.partial(_outproj_ln_kernel, alpha=alpha)
    row_spec = pl.BlockSpec((tm, d), lambda i: (i, 0))
    vec_spec = pl.BlockSpec((None, 1, d), lambda i: (layer, 0, 0))
    return pl.pallas_call(
        kern,
        grid=(n // tm,),
        in_specs=[pl.BlockSpec((tm, half), lambda i: (i, 0)),
                  pl.BlockSpec((tm, half), lambda i: (i, 0)),
                  row_spec,
                  pl.BlockSpec((half, d), lambda i: (0, 0)),
                  pl.BlockSpec((half, d), lambda i: (1, 0)),
                  vec_spec, vec_spec],
        out_specs=row_spec,
        out_shape=jax.ShapeDtypeStruct((n, d), F32),
        compiler_params=_params(1, 48),
        name="outproj_ln",
    )(fox_out, diff_out, xf, w_o, w_o, ln_g, ln_b)


def _split_bf16(a):
    hi = a.astype(BF16)
    lo = (a - hi.astype(F32)).astype(BF16)
    return hi, lo


def _router_kernel(x_ref, w_ref, b_ref, gate_ref, eid_ref):
    x_hi, x_lo = _split_bf16(x_ref[...])
    w_hi, w_lo = _split_bf16(w_ref[...])
    dg = functools.partial(lax.dot_general, dimension_numbers=NT_DIMS, preferred_element_type=F32)
    lg = dg(w_hi, x_hi) + dg(w_hi, x_lo) + dg(w_lo, x_hi) + b_ref[...]

    def first_index(vals, top):
        idx = jnp.full(top.shape, len(vals) - 1, I32)
        for j in range(len(vals) - 2, -1, -1):
            idx = jnp.where(vals[j] == top, j, idx)
        return idx

    g = [lg[i:i + 1, :] for i in range(N_GROUPS)]
    gmax = functools.reduce(jnp.maximum, g)
    gsum = functools.reduce(lambda a, b: a + b, [jnp.exp(gi - gmax) for gi in g])
    g_w = 1.0 / gsum
    sel = first_index(g, gmax)

    es = []
    for j in range(EXPERTS_PER_GROUP):
        rows = [lg[N_GROUPS + gi * EXPERTS_PER_GROUP + j:N_GROUPS + gi * EXPERTS_PER_GROUP + j + 1, :]
                for gi in range(N_GROUPS)]
        e = rows[N_GROUPS - 1]
        for gi in range(N_GROUPS - 2, -1, -1):
            e = jnp.where(sel == gi, rows[gi], e)
        es.append(e)
    emax = functools.reduce(jnp.maximum, es)
    ex = [jnp.exp(e - emax) for e in es]
    esum = functools.reduce(lambda a, b: a + b, ex)
    prob = [e / esum for e in ex]
    v1 = functools.reduce(jnp.maximum, prob)
    i1 = first_index(prob, v1)
    rest = [jnp.where(i1 == j, -1.0, prob[j]) for j in range(EXPERTS_PER_GROUP)]
    v2 = functools.reduce(jnp.maximum, rest)
    i2 = first_index(rest, v2)
    den = v1 + v2
    gate_ref[...] = jnp.zeros(gate_ref.shape, F32)
    eid_ref[...] = jnp.zeros(eid_ref.shape, I32)
    gate_ref[0:1, :] = g_w * (v1 / den)
    gate_ref[1:2, :] = g_w * (v2 / den)
    eid_ref[0:1, :] = sel * EXPERTS_PER_GROUP + i1
    eid_ref[1:2, :] = sel * EXPERTS_PER_GROUP + i2


def _router(xf, w_rt, b_r):
    n, d = xf.shape
    rows = w_rt.shape[0]
    tm = 512
    return pl.pallas_call(
        _router_kernel,
        grid=(n // tm,),
        in_specs=[pl.BlockSpec((tm, d), lambda i: (i, 0)),
                  pl.BlockSpec((rows, d), lambda i: (0, 0)),
                  pl.BlockSpec((rows, 1), lambda i: (0, 0))],
        out_specs=[pl.BlockSpec((8, tm), lambda i: (0, i)),
                   pl.BlockSpec((8, tm), lambda i: (0, i))],
        out_shape=[jax.ShapeDtypeStruct((8, n), F32), jax.ShapeDtypeStruct((8, n), I32)],
        compiler_params=_params(1, 32),
        name="router",
    )(xf, w_rt, b_r)


def _dispatch_kernel(pos_ref, x_ref, xs_ref, sem, *, tm):
    for r in range(tm):
        for k in range(TOP_K):
            pltpu.make_async_copy(x_ref.at[pl.ds(r, 1), :], xs_ref.at[pl.ds(pos_ref[k, r], 1), :], sem).start()
    for k in range(TOP_K):
        pltpu.make_async_copy(x_ref, xs_ref.at[pl.ds(0, tm), :], sem).wait()


def _dispatch(pos_tiles, xf):
    n, d = xf.shape
    tm = pos_tiles.shape[2]
    kern = functools.partial(_dispatch_kernel, tm=tm)
    return pl.pallas_call(
        kern,
        grid=(n // tm,),
        in_specs=[pl.BlockSpec((None, TOP_K, tm), lambda i: (i, 0, 0), memory_space=pltpu.SMEM),
                  pl.BlockSpec((tm, d), lambda i: (i, 0))],
        out_specs=pl.BlockSpec(memory_space=pl.ANY),
        out_shape=jax.ShapeDtypeStruct((TOP_K * n, d), F32),
        scratch_shapes=[pltpu.SemaphoreType.DMA(())],
        compiler_params=_params(1, 32),
        name="dispatch",
    )(pos_tiles, xf)


def _expert_kernel(chunk_ref, expert_ref, lo_ref, hi_ref, first_ref, nitems_ref,
                   xs_ref, wg_ref, wu_ref, wd_ref, ys_ref):
    i = pl.program_id(0)

    @pl.when(i < nitems_ref[0])
    def _():
        xb = xs_ref[...].astype(BF16)
        g = jnp.dot(xb, wg_ref[...].astype(BF16), preferred_element_type=F32)
        u = jnp.dot(xb, wu_ref[...].astype(BF16), preferred_element_type=F32)
        hid = (g * (1.0 / (1.0 + jnp.exp(-g)))) * u
        y = jnp.dot(hid.astype(BF16), wd_ref[...].astype(BF16), preferred_element_type=F32)
        row = lax.broadcasted_iota(I32, y.shape, 0)
        mine = jnp.logical_and(row >= lo_ref[i], row < hi_ref[i])

        @pl.when(first_ref[i] == 1)
        def _():
            ys_ref[...] = jnp.where(mine, y, 0.0)

        @pl.when(first_ref[i] == 0)
        def _():
            ys_ref[...] = jnp.where(mine, y, ys_ref[...])


def _experts(meta, xs, w_gate, w_up, w_down, layer):
    rows, d = xs.shape
    de = w_gate.shape[3]
    c = EXPERT_CHUNK
    n_items = meta[0].shape[0]
    grid_spec = pltpu.PrefetchScalarGridSpec(
        num_scalar_prefetch=6,
        grid=(n_items,),
        in_specs=[pl.BlockSpec((c, d), lambda i, ch, ex, lo, hi, fi, ni: (ch[i], 0)),
                  pl.BlockSpec((None, None, d, de), lambda i, ch, ex, lo, hi, fi, ni: (layer, ex[i], 0, 0)),
                  pl.BlockSpec((None, None, d, de), lambda i, ch, ex, lo, hi, fi, ni: (layer, ex[i], 0, 0)),
                  pl.BlockSpec((None, None, de, d), lambda i, ch, ex, lo, hi, fi, ni: (layer, ex[i], 0, 0))],
        out_specs=pl.BlockSpec((c, d), lambda i, ch, ex, lo, hi, fi, ni: (ch[i], 0)),
    )
    return pl.pallas_call(
        _expert_kernel,
        grid_spec=grid_spec,
        out_shape=jax.ShapeDtypeStruct((rows, d), F32),
        compiler_params=_params(1, 56),
        name="experts",
    )(*meta, xs, w_gate, w_up, w_down)


def _combine_ln_kernel(pos_ref, posn_ref, x_ref, g0_ref, g1_ref, gam_ref, bet_ref, ys_ref, o_ref, ob_ref,
                       ya0_ref, ya1_ref, yb0_ref, yb1_ref, sems, *, tm, alpha):
    i = pl.program_id(0)
    n_tiles = pl.num_programs(0)
    sets = ((ya0_ref, ya1_ref), (yb0_ref, yb1_ref))

    def issue(p_ref, which):
        for r in range(tm):
            for k in range(TOP_K):
                pltpu.make_async_copy(ys_ref.at[pl.ds(p_ref[k, r], 1), :], sets[which][k].at[pl.ds(r, 1), :],
                                      sems.at[which]).start()

    def finish(which):
        for k in range(TOP_K):
            pltpu.make_async_copy(ys_ref.at[pl.ds(0, tm), :], sets[which][k], sems.at[which]).wait()

    def compute(which):
        reps = x_ref.shape[1] // LANES
        ffn = _lane_tile(g0_ref[...], reps, axis=1) * sets[which][0][...]
        ffn = ffn + _lane_tile(g1_ref[...], reps, axis=1) * sets[which][1][...]
        out = _layer_norm_rows(alpha * x_ref[...] + ffn, gam_ref[...], bet_ref[...])
        o_ref[...] = out
        ob_ref[...] = out.astype(BF16)

    @pl.when(i == 0)
    def _():
        issue(pos_ref, 0)

    for which in range(2):
        @pl.when(i % 2 == which)
        def _(which=which):
            finish(which)
            issue(posn_ref, 1 - which)
            compute(which)

            @pl.when(i == n_tiles - 1)
            def _():
                finish(1 - which)


def _combine_ln(pos_tiles, xf, gate0, gate1, ys, ln_g, ln_b, layer, alpha):
    n, d = xf.shape
    n_tiles, _, tm = pos_tiles.shape
    kern = functools.partial(_combine_ln_kernel, tm=tm, alpha=alpha)
    row_spec = pl.BlockSpec((tm, d), lambda i: (i, 0))
    gate_spec = pl.BlockSpec((tm, LANES), lambda i: (i, 0))
    vec_spec = pl.BlockSpec((None, 1, d), lambda i: (layer, 0, 0))
    buf = pltpu.VMEM((tm, d), F32)
    return pl.pallas_call(
        kern,
        grid=(n_tiles,),
        in_specs=[pl.BlockSpec((None, TOP_K, tm), lambda i: (i, 0, 0), memory_space=pltpu.SMEM),
                  pl.BlockSpec((None, TOP_K, tm), lambda i: (jnp.minimum(i + 1, n_tiles - 1), 0, 0),
                               memory_space=pltpu.SMEM),
                  row_spec, gate_spec, gate_spec, vec_spec, vec_spec,
                  pl.BlockSpec(memory_space=pl.ANY)],
        out_specs=[row_spec, row_spec],
        out_shape=[jax.ShapeDtypeStruct((n, d), F32), jax.ShapeDtypeStruct((n, d), BF16)],
        scratch_shapes=[buf, buf, buf, buf, pltpu.SemaphoreType.DMA((2,))],
        compiler_params=_params(1, 48),
        name="combine_ln",
    )(pos_tiles, pos_tiles, xf, gate0, gate1, ln_g, ln_b, ys)


def _routing_tables(eid, n_items):
    flat = eid.reshape(-1)
    c = EXPERT_CHUNK
    onehot = (flat[:, None] == jnp.arange(N_EXPERTS, dtype=I32)[None, :]).astype(I32)
    csum = jnp.cumsum(onehot, axis=0)
    counts = csum[-1]
    ends = jnp.cumsum(counts)
    starts = ends - counts
    rank = jnp.sum(csum * onehot, axis=1) - 1
    pos = jnp.sum(onehot * starts[None, :], axis=1) + rank

    first_chunk = starts // c
    last_chunk = (ends - 1) // c
    items_per = jnp.where(counts > 0, last_chunk - first_chunk + 1, 0)
    item_ends = jnp.cumsum(items_per)
    total = item_ends[-1]
    it = jnp.minimum(jnp.arange(n_items, dtype=I32), total - 1)
    ex = jnp.sum((it[:, None] >= item_ends[None, :]).astype(I32), axis=1)
    ex_oh = (ex[:, None] == jnp.arange(N_EXPERTS, dtype=I32)[None, :]).astype(I32)
    pick = lambda v: jnp.sum(ex_oh * v[None, :], axis=1)
    chunk = pick(first_chunk) + it - (pick(item_ends) - pick(items_per))
    lo = jnp.maximum(pick(starts), chunk * c) - chunk * c
    hi = jnp.minimum(pick(ends), (chunk + 1) * c) - chunk * c
    prev_chunk = jnp.concatenate([jnp.full((1,), -1, I32), chunk[:-1]])
    first = (chunk != prev_chunk).astype(I32)
    meta = (chunk.astype(I32), ex.astype(I32), lo.astype(I32), hi.astype(I32), first,
            total.reshape(1).astype(I32))
    return pos.reshape(eid.shape).astype(I32), meta


def _rope_tables(seq):
    pos = jnp.arange(seq, dtype=F32)
    inv_freq = ROPE_THETA ** (-jnp.arange(0, ROPE_DIM, 2, dtype=F32) / ROPE_DIM)
    ang = pos[:, None] * inv_freq[None, :]
    cos, sin = jnp.cos(ang), jnp.sin(ang)
    ones = jnp.ones((seq, HEAD_DIM - ROPE_DIM), F32)
    zeros_half = jnp.zeros((seq, ROPE_HALF), F32)
    zeros_rest = jnp.zeros((seq, HEAD_DIM - ROPE_DIM), F32)
    c_tab = jnp.concatenate([cos, cos, ones], axis=1)
    sn_tab = jnp.concatenate([-sin, zeros_half, zeros_rest], axis=1)
    sp_tab = jnp.concatenate([zeros_half, sin, zeros_rest], axis=1)
    return c_tab, sn_tab, sp_tab


def kernel(x, w_in, b_f, diff_lambda, diff_subln_g, w_o, ln1_g, ln1_b, w_router_group,
           b_router_group, w_router_expert, b_router_expert, w_gate, w_up, w_down, ln2_g, ln2_b):
    batch, seq, d = x.shape
    depth = w_in.shape[0]
    n = batch * seq
    alpha = (2.0 * depth) ** 0.25
    fox_w = FOX_HEADS * HEAD_DIM
    diff_qk_w = 2 * DIFF_HEADS * HEAD_DIM
    gate_lo = 3 * fox_w
    gate_hi = gate_lo + FOX_HEADS
    q_blk = 2 * FOX_HEADS
    k_blk = q_blk + 2 * DIFF_HEADS
    rope_cols = (2 * fox_w, 2 * fox_w + 2 * diff_qk_w)

    rope_c, rope_sn, rope_sp = _rope_tables(seq)
    xf = x.reshape(n, d)
    xb = xf.astype(BF16)
    n_items = (TOP_K * n) // EXPERT_CHUNK + N_EXPERTS
    tile = 256

    for layer in range(depth):
        lambda_init = 0.8 - 0.6 * math.exp(-0.3 * layer)
        wl = w_in[layer]
        w_qk = jnp.concatenate([wl[:, :2 * fox_w], wl[:, gate_hi:gate_hi + 2 * diff_qk_w]], axis=1).astype(BF16)
        w_vt = jnp.concatenate([wl[:, 2 * fox_w:gate_lo], wl[:, gate_hi + 2 * diff_qk_w:]], axis=1).T.astype(BF16)
        w_f = jnp.pad(wl[:, gate_lo:gate_hi], ((0, 0), (0, LANES - FOX_HEADS))).astype(BF16)
        bias_f = jnp.pad(b_f[layer], (0, LANES - FOX_HEADS)).reshape(1, LANES)

        proj = _inproj(xb, w_qk, rope_c, rope_sn, rope_sp, seq, rope_cols)
        v_t = _inproj_t(xb, w_vt)
        qaug, kaug = _gate(xb, w_f, bias_f, seq)
        fox_out = _fox_attention(proj, qaug, kaug, v_t, batch, seq)
        diff_out = _diff_attention(proj, v_t, diff_lambda, diff_subln_g.reshape(depth, 1, -1), layer,
                                   lambda_init, batch, seq, q_blk, k_blk)
        x1 = _outproj_ln(fox_out, diff_out, xf, w_o[layer].astype(BF16),
                         ln1_g.reshape(depth, 1, d), ln1_b.reshape(depth, 1, d), layer, alpha)

        w_rt = jnp.concatenate([w_router_group[layer], w_router_expert[layer]], axis=1).T
        w_rt = jnp.pad(w_rt, ((0, 4), (0, 0)))
        b_r = jnp.pad(jnp.concatenate([b_router_group[layer], b_router_expert[layer]]), (0, 4)).reshape(-1, 1)
        gate8, eid8 = _router(x1, w_rt, b_r)
        pos, meta = _routing_tables(eid8[:TOP_K], n_items)
        pos_tiles = pos.reshape(TOP_K, n // tile, tile).transpose(1, 0, 2)
        xs = _dispatch(pos_tiles, x1)
        ys = _experts(meta, xs, w_gate, w_up, w_down, layer)
        gate0 = jnp.broadcast_to(gate8[0][:, None], (n, LANES))
        gate1 = jnp.broadcast_to(gate8[1][:, None], (n, LANES))
        xf, xb = _combine_ln(pos_tiles, x1, gate0, gate1, ys, ln2_g.reshape(depth, 1, d),
                             ln2_b.reshape(depth, 1, d), layer, alpha)
    return xf.reshape(batch, seq, d)
```
